```python
import jax
import jax.numpy as jnp
from jax import lax
import numpy as np

D_MODEL = 1024
BATCH = 8
SEQ = 2048
DEPTH = 4


HEAD_DIM = 64
D_MIX = D_MODEL
N_MIX_HEADS = D_MIX // HEAD_DIM
NSA_HEADS = 8
NSA_GROUPS = 2
NSA_HPG = NSA_HEADS // NSA_GROUPS
NSA_WIDTH = NSA_HEADS * HEAD_DIM
NSA_KV_WIDTH = NSA_GROUPS * HEAD_DIM
CMP_BLOCK = 32
CMP_STRIDE = 16
CMP_HIDDEN = 256
SEL_BLOCK = 64
SEL_TOPK = 16
SEL_LOCAL = 2
SEL_Q_BLOCK = 64
WINDOW = 512
WIN_Q_BLOCK = 128
FORCE_BONUS = 1.0e4
LRU_BLOCKS = 4
LRU_WIDTH = LRU_BLOCKS * HEAD_DIM
LRU_BW = LRU_WIDTH // LRU_BLOCKS
CONV_WIDTH = 4
LRU_C = 8.0
ML_HEADS = 4
ML_WIDTH = ML_HEADS * HEAD_DIM
ML_CHUNK = 64
D_FF = 2816
EPS = 1e-6
NEG = -1e30

IN_SPLITS = (NSA_WIDTH,
             NSA_KV_WIDTH, NSA_KV_WIDTH,
             NSA_KV_WIDTH, NSA_KV_WIDTH,
             NSA_KV_WIDTH, NSA_KV_WIDTH,
             NSA_HEADS * 3,
             LRU_WIDTH, LRU_WIDTH,
             ML_WIDTH, ML_WIDTH, ML_WIDTH,
             ML_HEADS, ML_HEADS,
             ML_WIDTH)
D_IN = sum(IN_SPLITS)
IN_SPLIT_POINTS = tuple(int(c) for c in np.cumsum(IN_SPLITS)[:-1])

kernel_name = 'hymba_nsa_rglru_mlstm_macaron'


def rms_norm(x, g):
    xf = x.astype(jnp.float32)
    y = xf * lax.rsqrt(jnp.mean(xf * xf, axis=-1, keepdims=True) + EPS)
    return (y * g.astype(jnp.float32)).astype(x.dtype)


def swiglu(x, w1, w3, w2):
    return (jax.nn.silu(x @ w1) * (x @ w3)) @ w2


def masked_softmax(s, mask):
    s = jnp.where(mask, s.astype(jnp.float32), NEG)
    p = jax.nn.softmax(s, axis=-1)
    return jnp.where(mask, p, 0.0)


def alibi_slopes(n_heads):
    return jnp.asarray(2.0 ** (-8.0 * np.arange(1, n_heads + 1) / n_heads), dtype=jnp.float32)


def compress_blocks(z, blk_idx, pos, w1, w2):
    B = z.shape[0]
    n_cmp = blk_idx.shape[0]
    blocks = jnp.transpose(z[:, blk_idx], (0, 1, 3, 2, 4)) + pos
    flat = blocks.reshape(B, n_cmp, NSA_GROUPS, CMP_BLOCK * HEAD_DIM)
    return jax.nn.silu(flat @ w1) @ w2


def nsa_mixer(q, k_cmp, v_cmp, k_slc, v_slc, k_win, v_win, gate_logits,
              pos_k, w1_k, w2_k, pos_v, w1_v, w2_v):
    B, S = q.shape[0], q.shape[1]
    G, R, dh = NSA_GROUPS, NSA_HPG, HEAD_DIM
    slopes = alibi_slopes(NSA_HEADS).reshape(G, R)
    qg = (q * dh ** -0.5).reshape(B, S, G, R, dh)
    t = jnp.arange(S)

    n_cmp = (S - CMP_BLOCK) // CMP_STRIDE + 1
    cmp_start = np.arange(n_cmp) * CMP_STRIDE
    cmp_end = cmp_start + CMP_BLOCK - 1
    blk_idx = cmp_start[:, None] + np.arange(CMP_BLOCK)[None, :]
    kc = compress_blocks(k_cmp, blk_idx, pos_k, w1_k, w2_k)
    vc = compress_blocks(v_cmp, blk_idx, pos_v, w1_v, w2_v)
    dist_c = t[:, None] - cmp_end[None, :]
    s_c = jnp.einsum('bsgrd,bngd->bgrsn', qg, kc) - slopes[:, :, None, None] * dist_c
    p_c = masked_softmax(s_c, dist_c >= 0)
    o_c = jnp.einsum('bgrsn,bngd->bsgrd', p_c.astype(vc.dtype), vc).reshape(B, S, NSA_HEADS, dh)

    n_slc = S // SEL_BLOCK
    n_sel = min(SEL_TOPK, n_slc)
    slc_start = np.arange(n_slc) * SEL_BLOCK
    overlap = ((cmp_start[:, None] <= slc_start[None, :] + SEL_BLOCK - 1)
               & (cmp_end[:, None] >= slc_start[None, :])).astype(np.float32)
    imp = jnp.einsum('bgrsn,nj->bgsj', p_c, jnp.asarray(overlap))
    cur = t // SEL_BLOCK
    j = jnp.arange(n_slc)
    valid = j[None, :] <= cur[:, None]
    forced = valid & ((j[None, :] == 0) | (j[None, :] > cur[:, None] - SEL_LOCAL))
    score = jnp.where(valid, imp + jnp.where(forced, FORCE_BONUS, 0.0), NEG)
    _, sel_idx = lax.top_k(score, n_sel)

    kb = jnp.transpose(k_slc.reshape(B, n_slc, SEL_BLOCK, G, dh), (0, 3, 1, 2, 4))
    vb = jnp.transpose(v_slc.reshape(B, n_slc, SEL_BLOCK, G, dh), (0, 3, 1, 2, 4))
    nqc = S // SEL_Q_BLOCK
    n_keys = n_sel * SEL_BLOCK
    q_ch = jnp.transpose(qg.reshape(B, nqc, SEL_Q_BLOCK, G, R, dh), (1, 0, 2, 3, 4, 5))
    i_ch = jnp.transpose(sel_idx.reshape(B, G, nqc, SEL_Q_BLOCK, n_sel), (2, 0, 1, 3, 4))
    t_ch = t.reshape(nqc, SEL_Q_BLOCK)
    gather = jax.vmap(jax.vmap(lambda blocks, ids: blocks[ids]))

    def sel_chunk(args):
        qc, ic, tc = args
        kg = gather(kb, ic).reshape(B, G, SEL_Q_BLOCK, n_keys, dh)
        vg = gather(vb, ic).reshape(B, G, SEL_Q_BLOCK, n_keys, dh)
        kpos = (ic[..., None] * SEL_BLOCK + jnp.arange(SEL_BLOCK)).reshape(B, G, SEL_Q_BLOCK, n_keys)
        dist = tc[None, None, :, None] - kpos
        s = jnp.einsum('bqgrd,bgqkd->bgrqk', qc, kg) - slopes[None, :, :, None, None] * dist[:, :, None]
        p = masked_softmax(s, (dist >= 0)[:, :, None])
        return jnp.einsum('bgrqk,bgqkd->bqgrd', p.astype(vg.dtype), vg)

    o_s = lax.map(sel_chunk, (q_ch, i_ch, t_ch))
    o_s = jnp.transpose(o_s, (1, 0, 2, 3, 4, 5)).reshape(B, S, NSA_HEADS, dh)

    nqb = S // WIN_Q_BLOCK
    span = WINDOW + WIN_Q_BLOCK
    win_idx = np.arange(nqb)[:, None] * WIN_Q_BLOCK + np.arange(span)[None, :]
    pad = ((0, 0), (WINDOW, 0), (0, 0), (0, 0))
    kw = jnp.pad(k_win, pad)[:, win_idx]
    vw = jnp.pad(v_win, pad)[:, win_idx]
    kpos_w = win_idx - WINDOW
    dist_w = t.reshape(nqb, WIN_Q_BLOCK)[:, :, None] - kpos_w[:, None, :]
    mask_w = (dist_w >= 0) & (dist_w < WINDOW) & (kpos_w[:, None, :] >= 0)
    qw = qg.reshape(B, nqb, WIN_Q_BLOCK, G, R, dh)
    s_w = jnp.einsum('bcqgrd,bckgd->bgrcqk', qw, kw) - slopes[:, :, None, None, None] * dist_w
    p_w = masked_softmax(s_w, mask_w)
    o_w = jnp.einsum('bgrcqk,bckgd->bcqgrd', p_w.astype(vw.dtype), vw).reshape(B, S, NSA_HEADS, dh)

    g = jax.nn.sigmoid(gate_logits.astype(jnp.float32)).astype(q.dtype)
    return g[..., 0:1] * o_c + g[..., 1:2] * o_s + g[..., 2:3] * o_w


def rglru_mixer(x, gate, conv_w, conv_b, w_a, b_a, w_x, b_x, lam):
    B, S, C = x.shape
    xp = jnp.pad(x, ((0, 0), (CONV_WIDTH - 1, 0), (0, 0)))
    xc = conv_b + xp[:, 0:S] * conv_w[0]
    for tap in range(1, CONV_WIDTH):
        xc = xc + xp[:, tap:tap + S] * conv_w[tap]
    xb = xc.reshape(B, S, LRU_BLOCKS, LRU_BW)
    r = jax.nn.sigmoid(jnp.einsum('bsnc,ncd->bsnd', xb, w_a) + b_a).reshape(B, S, C)
    i = jax.nn.sigmoid(jnp.einsum('bsnc,ncd->bsnd', xb, w_x) + b_x).reshape(B, S, C)
    log_a = -LRU_C * r.astype(jnp.float32) * jax.nn.softplus(-lam.astype(jnp.float32))
    a = jnp.exp(log_a)
    u = jnp.sqrt(-jnp.expm1(2.0 * log_a)) * (i * xc).astype(jnp.float32)

    def combine(left, right):
        a1, b1 = left
        a2, b2 = right
        return a1 * a2, a2 * b1 + b2

    _, h = lax.associative_scan(combine, (a, u), axis=1)
    return h.astype(x.dtype) * jax.nn.gelu(gate)


def mlstm_mixer(q, k, v, i_pre, f_pre, o_pre):
    B, S, NH, dh = q.shape
    L = ML_CHUNK
    NC = S // L
    f32 = jnp.float32

    def to_chunks(z):
        z = z.astype(f32).reshape((B, NC, L) + z.shape[2:])
        return jnp.moveaxis(z, 3, 1)

    qc = to_chunks(q)
    kc = to_chunks(k) * dh ** -0.5
    vc = to_chunks(v)
    ig = to_chunks(i_pre)
    lf = jax.nn.log_sigmoid(to_chunks(f_pre))
    a = jnp.cumsum(lf, axis=-1)
    A = a[..., -1]
    causal = np.tril(np.ones((L, L), dtype=bool))
    D = jnp.where(causal, a[..., :, None] - a[..., None, :] + ig[..., None, :], NEG)
    w_end = A[..., None] - a + ig

    def step(carry, xs):
        C, n, m = carry
        A_c, w_c, k_c, v_c = xs
        m_new = jnp.maximum(A_c + m, jnp.max(w_c, axis=-1))
        decay = jnp.exp(A_c + m - m_new)
        wk = jnp.exp(w_c - m_new[..., None])
        C_new = decay[..., None, None] * C + jnp.einsum('bhs,bhsd,bhse->bhde', wk, k_c, v_c)
        n_new = decay[..., None] * n + jnp.einsum('bhs,bhsd->bhd', wk, k_c)
        return (C_new, n_new, m_new), (C, n, m)

    init = (jnp.zeros((B, NH, dh, dh), f32), jnp.zeros((B, NH, dh), f32), jnp.zeros((B, NH), f32))
    xs = (jnp.moveaxis(A, 2, 0), jnp.moveaxis(w_end, 2, 0), jnp.moveaxis(kc, 2, 0), jnp.moveaxis(vc, 2, 0))
    _, (C_prev, n_prev, m_prev) = lax.scan(step, init, xs)
    C_prev = jnp.moveaxis(C_prev, 0, 2)
    n_prev = jnp.moveaxis(n_prev, 0, 2)
    m_prev = jnp.moveaxis(m_prev, 0, 2)

    m_inter = a + m_prev[..., None]
    m = jnp.maximum(m_inter, jnp.max(D, axis=-1))
    inter = jnp.exp(m_inter - m)
    W = jnp.exp(D - m[..., None]) * jnp.einsum('bhcjd,bhcsd->bhcjs', qc, kc)
    num = inter[..., None] * jnp.einsum('bhcjd,bhcde->bhcje', qc, C_prev) + jnp.einsum('bhcjs,bhcse->bhcje', W, vc)
    den = inter * jnp.einsum('bhcjd,bhcd->bhcj', qc, n_prev) + jnp.sum(W, axis=-1)
    h = num / jnp.maximum(jnp.abs(den), jnp.exp(-m))[..., None]
    h = jnp.moveaxis(h, 1, 3).reshape(B, S, NH * dh)
    return (jax.nn.sigmoid(o_pre.astype(f32)) * h).astype(q.dtype)


def setup_inputs(seed: int = 0) -> dict:
    key = jax.random.key(seed)
    ks = jax.random.split(key, 32)

    def nrm(i, shape, scale):
        return jax.random.normal(ks[i], shape, jnp.float32) * scale

    def gain(i, shape):
        return 1.0 + 0.01 * jax.random.normal(ks[i], shape, jnp.float32)

    a0 = jax.random.uniform(ks[19], (DEPTH, LRU_WIDTH), jnp.float32, minval=0.9, maxval=0.999)
    b_f = jnp.linspace(3.0, 6.0, ML_HEADS, dtype=jnp.float32)[None, :] + nrm(21, (DEPTH, ML_HEADS), 0.01)
    return {
        'x': nrm(0, (BATCH, SEQ, D_MODEL), 1.0),
        'ffn1_norm': gain(1, (DEPTH, D_MODEL)),
        'ffn1_w1': nrm(2, (DEPTH, D_MODEL, D_FF), D_MODEL ** -0.5),
        'ffn1_w3': nrm(3, (DEPTH, D_MODEL, D_FF), D_MODEL ** -0.5),
        'ffn1_w2': nrm(4, (DEPTH, D_FF, D_MODEL), D_FF ** -0.5),
        'mix_norm': gain(5, (DEPTH, D_MODEL)),
        'w_in': nrm(6, (DEPTH, D_MODEL, D_IN), D_MODEL ** -0.5),
        'nsa_cmp_pos_k': nrm(7, (DEPTH, CMP_BLOCK, HEAD_DIM), 0.1),
        'nsa_cmp_w1_k': nrm(8, (DEPTH, CMP_BLOCK * HEAD_DIM, CMP_HIDDEN), (CMP_BLOCK * HEAD_DIM) ** -0.5),
        'nsa_cmp_w2_k': nrm(9, (DEPTH, CMP_HIDDEN, HEAD_DIM), CMP_HIDDEN ** -0.5),
        'nsa_cmp_pos_v': nrm(10, (DEPTH, CMP_BLOCK, HEAD_DIM), 0.1),
        'nsa_cmp_w1_v': nrm(11, (DEPTH, CMP_BLOCK * HEAD_DIM, CMP_HIDDEN), (CMP_BLOCK * HEAD_DIM) ** -0.5),
        'nsa_cmp_w2_v': nrm(12, (DEPTH, CMP_HIDDEN, HEAD_DIM), CMP_HIDDEN ** -0.5),
        'lru_conv_w': nrm(13, (DEPTH, CONV_WIDTH, LRU_WIDTH), CONV_WIDTH ** -0.5),
        'lru_conv_b': nrm(14, (DEPTH, LRU_WIDTH), 0.02),
        'lru_w_a': nrm(15, (DEPTH, LRU_BLOCKS, LRU_BW, LRU_BW), LRU_BW ** -0.5),
        'lru_b_a': nrm(16, (DEPTH, LRU_BLOCKS, LRU_BW), 0.02),
        'lru_w_x': nrm(17, (DEPTH, LRU_BLOCKS, LRU_BW, LRU_BW), LRU_BW ** -0.5),
        'lru_b_x': nrm(18, (DEPTH, LRU_BLOCKS, LRU_BW), 0.02),
        'lru_lambda': jnp.log(a0) - jnp.log1p(-a0),
        'ml_b_i': nrm(20, (DEPTH, ML_HEADS), 0.02),
        'ml_b_f': b_f,
        'head_norm': gain(22, (DEPTH, N_MIX_HEADS, HEAD_DIM)),
        'w_out': nrm(23, (DEPTH, D_MIX, D_MODEL), D_MIX ** -0.5),
        'ffn2_norm': gain(24, (DEPTH, D_MODEL)),
        'ffn2_w1': nrm(25, (DEPTH, D_MODEL, D_FF), D_MODEL ** -0.5),
        'ffn2_w3': nrm(26, (DEPTH, D_MODEL, D_FF), D_MODEL ** -0.5),
        'ffn2_w2': nrm(27, (DEPTH, D_FF, D_MODEL), D_FF ** -0.5),
        'final_norm': gain(28, (D_MODEL,)),
    }


def reference(x, ffn1_norm, ffn1_w1, ffn1_w3, ffn1_w2, mix_norm, w_in,
              nsa_cmp_pos_k, nsa_cmp_w1_k, nsa_cmp_w2_k, nsa_cmp_pos_v, nsa_cmp_w1_v, nsa_cmp_w2_v,
              lru_conv_w, lru_conv_b, lru_w_a, lru_b_a, lru_w_x, lru_b_x, lru_lambda,
              ml_b_i, ml_b_f, head_norm, w_out, ffn2_norm, ffn2_w1, ffn2_w3, ffn2_w2, final_norm):
    B, S = x.shape[0], x.shape[1]
    h = x
    for l in range(DEPTH):
        h = h + 0.5 * swiglu(rms_norm(h, ffn1_norm[l]), ffn1_w1[l], ffn1_w3[l], ffn1_w2[l])

        u = rms_norm(h, mix_norm[l]) @ w_in[l]
        (q_n, kc_n, vc_n, ks_n, vs_n, kw_n, vw_n, g_n,
         x_r, g_r, q_m, k_m, v_m, i_m, f_m, o_m) = jnp.split(u, IN_SPLIT_POINTS, axis=-1)
        kv = lambda z: z.reshape(B, S, NSA_GROUPS, HEAD_DIM)
        o_nsa = nsa_mixer(q_n.reshape(B, S, NSA_HEADS, HEAD_DIM), kv(kc_n), kv(vc_n), kv(ks_n), kv(vs_n),
                          kv(kw_n), kv(vw_n), g_n.reshape(B, S, NSA_HEADS, 3),
                          nsa_cmp_pos_k[l], nsa_cmp_w1_k[l], nsa_cmp_w2_k[l],
                          nsa_cmp_pos_v[l], nsa_cmp_w1_v[l], nsa_cmp_w2_v[l])
        o_lru = rglru_mixer(x_r, g_r, lru_conv_w[l], lru_conv_b[l], lru_w_a[l], lru_b_a[l],
                            lru_w_x[l], lru_b_x[l], lru_lambda[l])
        mh = lambda z: z.reshape(B, S, ML_HEADS, HEAD_DIM)
        o_ml = mlstm_mixer(mh(q_m), mh(k_m), mh(v_m), i_m + ml_b_i[l], f_m + ml_b_f[l], o_m)
        heads = jnp.concatenate([o_nsa, o_lru.reshape(B, S, LRU_BLOCKS, HEAD_DIM),
                                 o_ml.reshape(B, S, ML_HEADS, HEAD_DIM)], axis=2)
        heads = rms_norm(heads, head_norm[l]).reshape(B, S, D_MIX)
        h = h + heads @ w_out[l]

        h = h + 0.5 * swiglu(rms_norm(h, ffn2_norm[l]), ffn2_w1[l], ffn2_w3[l], ffn2_w2[l])
    return rms_norm(h, final_norm)
```

```python
import functools

import jax
import jax.numpy as jnp
import numpy as np
from jax import lax
from jax.experimental import pallas as pl
from jax.experimental.pallas import tpu as pltpu

F32 = jnp.float32
BF16 = jnp.bfloat16

D_MODEL = 1024
DEPTH = 4
HEAD_DIM = 64
NSA_HEADS = 8
NSA_GROUPS = 2
NSA_HPG = NSA_HEADS // NSA_GROUPS
NSA_WIDTH = NSA_HEADS * HEAD_DIM
NSA_KV_WIDTH = NSA_GROUPS * HEAD_DIM
CMP_BLOCK = 32
CMP_STRIDE = 16
CMP_HIDDEN = 256
SEL_BLOCK = 64
SEL_TOPK = 16
SEL_LOCAL = 2
WINDOW = 512
FORCE_BONUS = 1.0e4
LRU_BLOCKS = 4
LRU_WIDTH = LRU_BLOCKS * HEAD_DIM
CONV_WIDTH = 4
LRU_C = 8.0
ML_HEADS = 4
ML_WIDTH = ML_HEADS * HEAD_DIM
ML_CHUNK = 64
D_FF = 2816
EPS = 1e-6
NEG = -1e30

LANES = 128
SUBLANES = 8
VMEM_LIMIT = 52 * 1024 * 1024

_OFF_Q = 0
_OFF_KV = NSA_WIDTH
_OFF_G = _OFF_KV + 6 * NSA_KV_WIDTH
_OFF_XR = _OFF_G + NSA_HEADS * 3
_OFF_GR = _OFF_XR + LRU_WIDTH
_OFF_QM = _OFF_GR + LRU_WIDTH
_OFF_KM = _OFF_QM + ML_WIDTH
_OFF_VM = _OFF_KM + ML_WIDTH
_OFF_IM = _OFF_VM + ML_WIDTH
_OFF_FM = _OFF_IM + ML_HEADS
_OFF_OM = _OFF_FM + ML_HEADS
D_IN = _OFF_OM + ML_WIDTH

N_SLOTS = 36
SLOT_Q, SLOT_KC, SLOT_VC, SLOT_KS, SLOT_VS, SLOT_KW, SLOT_VW = 0, 8, 10, 12, 14, 16, 18
SLOT_QM, SLOT_KM, SLOT_VM, SLOT_OM = 20, 24, 28, 32
GATE_I, GATE_F = NSA_HEADS * 3, NSA_HEADS * 3 + ML_HEADS
N_T_ROWS = ML_WIDTH + 2 * ML_HEADS


def _cparams(sem):
    return pltpu.CompilerParams(dimension_semantics=sem, vmem_limit_bytes=VMEM_LIMIT)


def _rms(x, g):
    return x * lax.rsqrt(jnp.mean(x * x, axis=-1, keepdims=True) + EPS) * g


def _split3(x):
    hi = x.astype(BF16)
    r1 = x - hi.astype(F32)
    mid = r1.astype(BF16)
    lo = (r1 - mid.astype(F32)).astype(BF16)
    return hi, mid, lo


def _dot_exact01(x, m01):
    hi, mid, lo = _split3(x)
    d = lambda a: jnp.dot(a, m01, preferred_element_type=F32)
    return d(hi) + d(mid) + d(lo)


def _ffn_body(h_ref, g_ref, w1_ref, w3_ref, w2_ref, fg_ref, o_ref, xn_ref, acc_ref, *, n_f, final):
    j = pl.program_id(1)

    @pl.when(j == 0)
    def _():
        xn_ref[...] = _rms(h_ref[...], g_ref[...]).astype(BF16)

    xn = xn_ref[...]
    a = jnp.dot(xn, w1_ref[0], preferred_element_type=F32)
    b = jnp.dot(xn, w3_ref[0], preferred_element_type=F32)
    gated = (a * jax.nn.sigmoid(a) * b).astype(BF16)
    part = jnp.dot(gated, w2_ref[0], preferred_element_type=F32)

    @pl.when(j == 0)
    def _():
        acc_ref[...] = part

    @pl.when(j > 0)
    def _():
        acc_ref[...] += part

    @pl.when(j == n_f - 1)
    def _():
        y = h_ref[...] + 0.5 * acc_ref[...]
        if final:
            y = _rms(y, fg_ref[...])
        o_ref[...] = y


def _ffn(h, gain, w1, w3, w2, layer, final_gain=None, *, tm=512, tf=1408):
    m, d = h.shape
    n_f = D_FF // tf
    final = final_gain is not None
    fg = final_gain if final else gain
    body = functools.partial(_ffn_body, n_f=n_f, final=final)
    return pl.pallas_call(
        body,
        grid=(m // tm, n_f),
        in_specs=[
            pl.BlockSpec((tm, d), lambda i, j: (i, 0)),
            pl.BlockSpec((1, d), lambda i, j: (0, 0)),
            pl.BlockSpec((1, d, tf), lambda i, j: (layer, 0, j)),
            pl.BlockSpec((1, d, tf), lambda i, j: (layer, 0, j)),
            pl.BlockSpec((1, tf, d), lambda i, j: (layer, j, 0)),
            pl.BlockSpec((1, d), lambda i, j: (0, 0)),
        ],
        out_specs=pl.BlockSpec((tm, d), lambda i, j: (i, 0)),
        out_shape=jax.ShapeDtypeStruct((m, d), F32),
        scratch_shapes=[pltpu.VMEM((tm, d), BF16), pltpu.VMEM((tm, d), F32)],
        compiler_params=_cparams(("parallel", "arbitrary")),
        name="ffn",
    )(h, gain, w1, w3, w2, fg)


def _inproj_body(h_ref, g_ref, w_ref, wt_ref, heads_ref, xr_ref, gr_ref, gates_ref, kt_ref, gt_ref, *, tm):
    xn = _rms(h_ref[0], g_ref[...]).astype(BF16)
    per = 2 * LANES // HEAD_DIM
    for c in range(N_SLOTS // per):
        u = jnp.dot(xn, w_ref[0, :, c * 2 * LANES:(c + 1) * 2 * LANES], preferred_element_type=F32)
        for i in range(per):
            heads_ref[0, c * per + i] = u[:, i * HEAD_DIM:(i + 1) * HEAD_DIM]
    base = N_SLOTS * HEAD_DIM
    xr_ref[0] = jnp.dot(xn, w_ref[0, :, base:base + LRU_WIDTH], preferred_element_type=F32)
    gr_ref[0] = jnp.dot(xn, w_ref[0, :, base + LRU_WIDTH:base + 2 * LRU_WIDTH], preferred_element_type=F32)
    gates_ref[0] = jnp.dot(xn, w_ref[0, :, base + 2 * LRU_WIDTH:], preferred_element_type=F32)
    ut = lax.dot_general(wt_ref[0], xn, (((1,), (1,)), ((), ())), preferred_element_type=F32)
    for cc in range(tm // ML_CHUNK):
        cols = slice(cc * ML_CHUNK, (cc + 1) * ML_CHUNK)
        for hh in range(ML_HEADS):
            kt_ref[0, hh, cc] = ut[hh * HEAD_DIM:(hh + 1) * HEAD_DIM, cols]
        gt_ref[0, cc] = ut[ML_WIDTH:ML_WIDTH + 2 * ML_HEADS, cols]


def _in_proj(h3, gain, w_main, w_t, layer, *, tm=256):
    b, s, d = h3.shape
    n_main = w_main.shape[-1]
    nc = s // ML_CHUNK
    body = functools.partial(_inproj_body, tm=tm)
    return pl.pallas_call(
        body,
        grid=(b, s // tm),
        in_specs=[
            pl.BlockSpec((1, tm, d), lambda bi, ti: (bi, ti, 0)),
            pl.BlockSpec((1, d), lambda bi, ti: (0, 0)),
            pl.BlockSpec((1, d, n_main), lambda bi, ti: (layer, 0, 0)),
            pl.BlockSpec((1, N_T_ROWS, d), lambda bi, ti: (layer, 0, 0)),
        ],
        out_specs=[
            pl.BlockSpec((1, N_SLOTS, tm, HEAD_DIM), lambda bi, ti: (bi, 0, ti, 0)),
            pl.BlockSpec((1, tm, LRU_WIDTH), lambda bi, ti: (bi, ti, 0)),
            pl.BlockSpec((1, tm, LRU_WIDTH), lambda bi, ti: (bi, ti, 0)),
            pl.BlockSpec((1, tm, LANES), lambda bi, ti: (bi, ti, 0)),
            pl.BlockSpec((1, ML_HEADS, tm // ML_CHUNK, HEAD_DIM, ML_CHUNK), lambda bi, ti: (bi, 0, ti, 0, 0)),
            pl.BlockSpec((1, tm // ML_CHUNK, 2 * ML_HEADS, ML_CHUNK), lambda bi, ti: (bi, ti, 0, 0)),
        ],
        out_shape=[
            jax.ShapeDtypeStruct((b, N_SLOTS, s, HEAD_DIM), F32),
            jax.ShapeDtypeStruct((b, s, LRU_WIDTH), F32),
            jax.ShapeDtypeStruct((b, s, LRU_WIDTH), F32),
            jax.ShapeDtypeStruct((b, s, LANES), F32),
            jax.ShapeDtypeStruct((b, ML_HEADS, nc, HEAD_DIM, ML_CHUNK), F32),
            jax.ShapeDtypeStruct((b, nc, 2 * ML_HEADS, ML_CHUNK), F32),
        ],
        compiler_params=_cparams(("parallel", "parallel")),
        name="in_proj",
    )(h3, gain, w_main, w_t)


def _compress_body(z_ref, pos_ref, w1_ref, w2_ref, o_ref):
    half = CMP_STRIDE * HEAD_DIM
    z = z_ref[0, 0]
    pos = pos_ref[0]
    top = (z + pos[:, :half]).astype(BF16)
    bot = (z + pos[:, half:]).astype(BF16)
    a = jnp.dot(top, w1_ref[0, :half, :], preferred_element_type=F32)
    b = jnp.dot(bot, w1_ref[0, half:, :], preferred_element_type=F32)
    n = z.shape[0]
    hid = a + pltpu.roll(b, n - 1, axis=0)
    hid = hid * jax.nn.sigmoid(hid)
    o_ref[0, 0] = jnp.dot(hid.astype(BF16), w2_ref[0], preferred_element_type=F32)


def _compress(heads_z, pos, w1, w2):
    b, _, n, w = heads_z.shape
    return pl.pallas_call(
        _compress_body,
        grid=(b, 2 * NSA_GROUPS),
        in_specs=[
            pl.BlockSpec((1, 1, n, w), lambda bi, j: (bi, SLOT_KC + j, 0, 0)),
            pl.BlockSpec((1, 1, 2 * w), lambda bi, j: (j // NSA_GROUPS, 0, 0)),
            pl.BlockSpec((1, 2 * w, CMP_HIDDEN), lambda bi, j: (j // NSA_GROUPS, 0, 0)),
            pl.BlockSpec((1, CMP_HIDDEN, HEAD_DIM), lambda bi, j: (j // NSA_GROUPS, 0, 0)),
        ],
        out_specs=pl.BlockSpec((1, 1, n, HEAD_DIM), lambda bi, j: (bi, (j % NSA_GROUPS) * 2 + j // NSA_GROUPS, 0, 0)),
        out_shape=jax.ShapeDtypeStruct((b, 2 * NSA_GROUPS, n, HEAD_DIM), F32),
        compiler_params=_cparams(("parallel", "parallel")),
        name="compress",
    )(heads_z, pos, w1, w2)


TQ = 128
N_SEL = 32


def _nsa_body(q_ref, cmp_ref, ks_ref, vs_ref, kw_ref, vw_ref, gate_ref, ov_ref, ex_ref, o_ref,
              m_sc, l_sc, acc_sc):
    g = pl.program_id(1)
    qi = pl.program_id(2)
    r_rows = NSA_HPG * TQ
    nt = (((1,), (1,)), ((), ()))

    q = (q_ref[0].reshape(r_rows, HEAD_DIM) * HEAD_DIM ** -0.5).astype(BF16)
    row = lax.broadcasted_iota(jnp.int32, (r_rows, 1), 0)
    head = g * NSA_HPG + row // TQ
    slope = jnp.exp2(-(head + 1).astype(F32) * (8.0 / NSA_HEADS))
    t_q = qi * TQ + lax.broadcasted_iota(jnp.int32, (TQ, 1), 0)
    lane = lax.broadcasted_iota(jnp.int32, (1, LANES), 1)

    def tile4(x):
        return jnp.concatenate([x] * NSA_HPG, axis=0)

    kc = cmp_ref[0, 0].astype(BF16)
    vc = cmp_ref[0, 1].astype(BF16)
    cmp_end = lane * CMP_STRIDE + (CMP_BLOCK - 1)
    s_c = lax.dot_general(q, kc, nt, preferred_element_type=F32) + slope * cmp_end.astype(F32)
    ok_c = tile4(cmp_end <= t_q)
    s_c = jnp.where(ok_c, s_c, NEG)
    m_c = jnp.max(s_c, axis=1, keepdims=True)
    e_c = jnp.where(ok_c, jnp.exp(s_c - m_c), 0.0)
    l_c = jnp.sum(e_c, axis=1, keepdims=True)
    p_c = e_c / jnp.where(l_c > 0.0, l_c, 1.0)
    o_c = jnp.dot(p_c.astype(BF16), vc, preferred_element_type=F32)

    p_sum = p_c[0:TQ]
    for r in range(1, NSA_HPG):
        p_sum = p_sum + p_c[r * TQ:(r + 1) * TQ]
    imp = _dot_exact01(p_sum, ov_ref[...])
    cur = t_q // SEL_BLOCK
    valid = lane <= cur
    forced = valid & ((lane == 0) | (lane > cur - SEL_LOCAL))
    score = jnp.where(valid, imp + jnp.where(forced, FORCE_BONUS, 0.0), NEG)
    rank = jnp.zeros((TQ, LANES), F32)
    for jp in range(N_SEL):
        col = score[:, jp:jp + 1]
        beats = (col > score) | ((col == score) & (lane > jp))
        rank = rank + jnp.where(beats, 1.0, 0.0)
    unsel = jnp.where((rank < float(SEL_TOPK)) & (lane < N_SEL), 0.0, 1.0).astype(BF16)

    def attend(k_ref, v_ref, kt, extra, keep):
        start = pl.multiple_of(kt * TQ, TQ)
        k = k_ref[0, 0, pl.ds(start, TQ), :].astype(BF16)
        v = v_ref[0, 0, pl.ds(start, TQ), :].astype(BF16)
        kpos = start + lane
        s = lax.dot_general(q, k, nt, preferred_element_type=F32) + slope * kpos.astype(F32)
        if extra is not None:
            s = s + tile4(extra(kt))
        if keep is not None:
            s = jnp.where(tile4(keep(kpos)), s, NEG)
        m_prev = m_sc[...]
        m_new = jnp.maximum(m_prev, jnp.max(s, axis=1, keepdims=True))
        alpha = jnp.exp(m_prev - m_new)
        p = jnp.exp(s - m_new)
        l_sc[...] = alpha * l_sc[...] + jnp.sum(p, axis=1, keepdims=True)
        acc_sc[...] = alpha * acc_sc[...] + jnp.dot(p.astype(BF16), v, preferred_element_type=F32)
        m_sc[...] = m_new

    def reset():
        m_sc[...] = jnp.full((r_rows, 1), NEG, F32)
        l_sc[...] = jnp.zeros((r_rows, 1), F32)
        acc_sc[...] = jnp.zeros((r_rows, HEAD_DIM), F32)

    def finish():
        return acc_sc[...] / l_sc[...]

    causal = lambda kpos: kpos <= t_q

    def sel_bias(kt):
        start = pl.multiple_of(kt * TQ, TQ)
        return jnp.dot(unsel, ex_ref[:, pl.ds(start, TQ)], preferred_element_type=F32)

    reset()

    def sel_loop(kt, carry):
        attend(ks_ref, vs_ref, kt, sel_bias, None)
        return carry

    lax.fori_loop(0, qi, sel_loop, 0)
    attend(ks_ref, vs_ref, qi, sel_bias, causal)
    o_s = finish()

    reset()
    n_back = WINDOW // TQ

    @pl.when(qi >= n_back)
    def _():
        attend(kw_ref, vw_ref, qi - n_back, None, lambda kpos: kpos > t_q - WINDOW)

    def win_loop(kt, carry):
        attend(kw_ref, vw_ref, kt, None, None)
        return carry

    lax.fori_loop(jnp.maximum(qi - n_back + 1, 0), qi, win_loop, 0)
    attend(kw_ref, vw_ref, qi, None, causal)
    o_w = finish()

    gates = jax.nn.sigmoid(gate_ref[0])
    outs = []
    for r in range(NSA_HPG):
        rows = slice(r * TQ, (r + 1) * TQ)
        acc = None
        for br, o_b in enumerate((o_c, o_s, o_w)):
            lane_lo = r * 3 + br
            g0 = gates[:, lane_lo:lane_lo + 1]
            g1 = gates[:, NSA_HPG * 3 + lane_lo:NSA_HPG * 3 + lane_lo + 1]
            gsel = jnp.where(g == 0, g0, g1)
            term = gsel * o_b[rows]
            acc = term if acc is None else acc + term
        outs.append(acc)
    o_ref[0] = jnp.concatenate(outs, axis=1)


def _nsa(heads, cmp_kv, gates, ov, ex):
    b, _, s, _ = heads.shape
    n_cmp = cmp_kv.shape[2]
    kv_spec = lambda slot: pl.BlockSpec((1, 1, s, HEAD_DIM), lambda bi, g, qi: (bi, slot + g, 0, 0))
    r_rows = NSA_HPG * TQ
    return pl.pallas_call(
        _nsa_body,
        grid=(b, NSA_GROUPS, s // TQ),
        in_specs=[
            pl.BlockSpec((1, NSA_HPG, TQ, HEAD_DIM), lambda bi, g, qi: (bi, g, qi, 0)),
            pl.BlockSpec((1, 2, n_cmp, HEAD_DIM), lambda bi, g, qi: (bi, g, 0, 0)),
            kv_spec(SLOT_KS), kv_spec(SLOT_VS), kv_spec(SLOT_KW), kv_spec(SLOT_VW),
            pl.BlockSpec((1, TQ, LANES), lambda bi, g, qi: (bi, qi, 0)),
            pl.BlockSpec((LANES, LANES), lambda bi, g, qi: (0, 0)),
            pl.BlockSpec((LANES, s), lambda bi, g, qi: (0, 0)),
        ],
        out_specs=pl.BlockSpec((1, TQ, NSA_HPG * HEAD_DIM), lambda bi, g, qi: (bi, qi, g)),
        out_shape=jax.ShapeDtypeStruct((b, s, NSA_WIDTH), F32),
        scratch_shapes=[
            pltpu.VMEM((r_rows, 1), F32),
            pltpu.VMEM((r_rows, 1), F32),
            pltpu.VMEM((r_rows, HEAD_DIM), F32),
        ],
        compiler_params=_cparams(("parallel", "parallel", "arbitrary")),
        name="nsa",
    )(heads, cmp_kv, heads, heads, heads, heads, gates, ov, ex)


SCAN_PAD = 1024
SCAN_ROWS = 256


def _lru_body(x_ref, g_ref, cw_ref, cb_ref, wa_ref, ba_ref, wx_ref, bx_ref, lam_ref, o_ref,
              a0, b0, a1, b1):
    s = x_ref.shape[1]
    x = x_ref[0]
    row = lax.broadcasted_iota(jnp.int32, x.shape, 0)
    xc = cb_ref[...]
    for tap in range(CONV_WIDTH):
        back = CONV_WIDTH - 1 - tap
        xs = x if back == 0 else jnp.where(row >= back, pltpu.roll(x, back, axis=0), 0.0)
        xc = xc + xs * cw_ref[tap:tap + 1, :]
    xb = xc.astype(BF16)
    r = jax.nn.sigmoid(jnp.dot(xb, wa_ref[...], preferred_element_type=F32) + ba_ref[...])
    i = jax.nn.sigmoid(jnp.dot(xb, wx_ref[...], preferred_element_type=F32) + bx_ref[...])
    log_a = -LRU_C * r * jax.nn.softplus(-lam_ref[...])
    a = jnp.exp(log_a)
    u = jnp.sqrt(-jnp.tanh(log_a) * (a * a + 1.0)) * (i * xc)

    bufs = ((a0, b0), (a1, b1))
    for ab, bb in bufs:
        ab[0:SCAN_PAD, :] = jnp.ones((SCAN_PAD, x.shape[1]), F32)
        bb[0:SCAN_PAD, :] = jnp.zeros((SCAN_PAD, x.shape[1]), F32)
    a0[SCAN_PAD:, :] = a
    b0[SCAN_PAD:, :] = u
    n_steps = int(np.log2(s))
    for k in range(n_steps):
        d = 1 << k
        (sa, sb), (da, db) = bufs[k % 2], bufs[(k + 1) % 2]
        for c in range(s // SCAN_ROWS):
            lo = SCAN_PAD + c * SCAN_ROWS
            a_cur = sa[lo:lo + SCAN_ROWS, :]
            a_prev = sa[lo - d:lo - d + SCAN_ROWS, :]
            b_cur = sb[lo:lo + SCAN_ROWS, :]
            b_prev = sb[lo - d:lo - d + SCAN_ROWS, :]
            da[lo:lo + SCAN_ROWS, :] = a_cur * a_prev
            db[lo:lo + SCAN_ROWS, :] = a_cur * b_prev + b_cur
    h = bufs[n_steps % 2][1][SCAN_PAD:, :]
    o_ref[0] = h * jax.nn.gelu(g_ref[0])


def _rglru(xr, gr, conv_w, conv_b, wa, ba, wx, bx, lam):
    b, s, c = xr.shape
    assert s <= 2 * SCAN_PAD and s % SCAN_ROWS == 0
    tok = pl.BlockSpec((1, s, c), lambda bi: (bi, 0, 0))
    vec = lambda n: pl.BlockSpec((n, c), lambda bi: (0, 0))
    return pl.pallas_call(
        _lru_body,
        grid=(b,),
        in_specs=[tok, tok, vec(CONV_WIDTH), vec(1), vec(c), vec(1), vec(c), vec(1), vec(1)],
        out_specs=tok,
        out_shape=jax.ShapeDtypeStruct((b, s, c), F32),
        scratch_shapes=[pltpu.VMEM((SCAN_PAD + s, c), F32)] * 4,
        compiler_params=_cparams(("parallel",)),
        name="rglru",
    )(xr, gr, conv_w, conv_b, wa, ba, wx, bx, lam)


def _mlstm_body(q_ref, v_ref, o_ref_in, kt_ref, gcol_ref, grow_ref, bcol_ref, brow_ref, tri_ref, out_ref):
    nc = q_ref.shape[2]
    L = ML_CHUNK
    ones_col = jnp.where(lax.broadcasted_iota(jnp.int32, (L, HEAD_DIM), 1) == 0, 1.0, 0.0)
    rr = lax.broadcasted_iota(jnp.int32, (L, L), 0)
    cc = lax.broadcasted_iota(jnp.int32, (L, L), 1)
    causal = cc <= rr
    srow = lax.broadcasted_iota(jnp.int32, (L, LANES), 0)

    def chunk(c, carry):
        gcol = gcol_ref[0, c] + bcol_ref[...]
        lf_col = jax.nn.log_sigmoid(gcol)
        a_col = lf_col
        for k in range(int(np.log2(L))):
            d = 1 << k
            a_col = a_col + jnp.where(srow >= d, pltpu.roll(a_col, d, axis=0), 0.0)
        grow = grow_ref[0, c] + brow_ref[...]
        lf_row = jax.nn.log_sigmoid(grow)
        a_rows = _dot_exact01(lf_row, tri_ref[...])
        new_carry = []
        outs = []
        for h in range(ML_HEADS):
            c_aug, m_prev = carry[h]
            q = q_ref[0, h, c].astype(BF16)
            kt = kt_ref[0, h, c]
            v = v_ref[0, h, c]
            v_aug = jnp.concatenate([v, ones_col], axis=1).astype(BF16)
            a_c = a_col[:, GATE_F + h:GATE_F + h + 1]
            a_r = jnp.broadcast_to(a_rows[ML_HEADS + h:ML_HEADS + h + 1, :], (SUBLANES, L))
            ig_r = jnp.broadcast_to(grow[h:h + 1, :], (SUBLANES, L))
            a_end = a_r[:, L - 1:L]
            tall = lambda s8: jnp.concatenate([s8] * (L // SUBLANES), axis=0)

            qk = jnp.dot(q, kt.astype(BF16), preferred_element_type=F32) * HEAD_DIM ** -0.5
            dmat = jnp.where(causal, a_c + (ig_r - a_r)[0:1, :], NEG)
            m_inter = a_c + tall(m_prev)
            m = jnp.maximum(m_inter, jnp.max(dmat, axis=1, keepdims=True))
            inter = jnp.exp(m_inter - m)
            w = jnp.exp(dmat - m) * qk
            num = inter * jnp.dot(q, c_aug.astype(BF16), preferred_element_type=F32)
            num = num + jnp.dot(w.astype(BF16), v_aug, preferred_element_type=F32)
            den = num[:, HEAD_DIM:HEAD_DIM + 1]
            hval = num[:, :HEAD_DIM] / jnp.maximum(jnp.abs(den), jnp.exp(-m))
            outs.append(jax.nn.sigmoid(o_ref_in[0, h, c]) * hval)

            w_end = a_end - a_r + ig_r
            m_new = jnp.maximum(a_end + m_prev, jnp.max(w_end, axis=1, keepdims=True))
            decay = jnp.exp(a_end + m_prev - m_new)
            wk = jnp.exp(w_end - m_new) * HEAD_DIM ** -0.5
            kw = (kt * wk[0:1, :]).astype(BF16)
            c_new = tall(decay) * c_aug + jnp.dot(kw, v_aug, preferred_element_type=F32)
            new_carry.append((c_new, m_new))
        out_ref[0, c] = jnp.concatenate(outs, axis=1)
        return tuple(new_carry)

    init = tuple((jnp.zeros((HEAD_DIM, 2 * HEAD_DIM), F32), jnp.zeros((SUBLANES, 1), F32)) for _ in range(ML_HEADS))
    lax.fori_loop(0, nc, chunk, init)


def _mlstm(heads_c, kt, gates_c, gt, bcol, brow, tri):
    b, _, nc, L, _ = heads_c.shape
    slot_spec = lambda slot: pl.BlockSpec((1, ML_HEADS, nc, L, HEAD_DIM), lambda bi: (bi, slot // ML_HEADS, 0, 0, 0))
    return pl.pallas_call(
        _mlstm_body,
        grid=(b,),
        in_specs=[
            slot_spec(SLOT_QM), slot_spec(SLOT_VM), slot_spec(SLOT_OM),
            pl.BlockSpec((1, ML_HEADS, nc, HEAD_DIM, L), lambda bi: (bi, 0, 0, 0, 0)),
            pl.BlockSpec((1, nc, L, LANES), lambda bi: (bi, 0, 0, 0)),
            pl.BlockSpec((1, nc, 2 * ML_HEADS, L), lambda bi: (bi, 0, 0, 0)),
            pl.BlockSpec((1, LANES), lambda bi: (0, 0)),
            pl.BlockSpec((2 * ML_HEADS, 1), lambda bi: (0, 0)),
            pl.BlockSpec((L, L), lambda bi: (0, 0)),
        ],
        out_specs=pl.BlockSpec((1, nc, L, ML_WIDTH), lambda bi: (bi, 0, 0, 0)),
        out_shape=jax.ShapeDtypeStruct((b, nc, L, ML_WIDTH), F32),
        compiler_params=_cparams(("parallel",)),
        name="mlstm",
    )(heads_c, heads_c, heads_c, kt, gates_c, gt, bcol, brow, tri)


def _outproj_body(h_ref, on_ref, ol_ref, om_ref, gn_ref, ind_ref, indt_ref, w_ref, o_ref):
    acc = h_ref[...]
    col = 0
    for x_ref in (on_ref, ol_ref, om_ref):
        x = x_ref[...]
        width = x.shape[1]
        ind = ind_ref[col:col + width, :]
        ms = _dot_exact01(x * x, ind) * (1.0 / HEAD_DIM)
        scale = _dot_exact01(lax.rsqrt(ms + EPS), indt_ref[:, col:col + width])
        xn = (x * scale * gn_ref[:, col:col + width]).astype(BF16)
        acc = acc + jnp.dot(xn, w_ref[0, col:col + width, :], preferred_element_type=F32)
        col += width
    o_ref[...] = acc


def _out_proj(h, o_nsa, o_lru, o_ml, gain, ind, indt, w_out, layer, *, tm=512):
    m, d = h.shape
    tok = lambda w: pl.BlockSpec((tm, w), lambda i: (i, 0))
    return pl.pallas_call(
        _outproj_body,
        grid=(m // tm,),
        in_specs=[
            tok(d), tok(o_nsa.shape[1]), tok(o_lru.shape[1]), tok(o_ml.shape[1]),
            pl.BlockSpec((1, d), lambda i: (0, 0)),
            pl.BlockSpec((d, LANES), lambda i: (0, 0)),
            pl.BlockSpec((LANES, d), lambda i: (0, 0)),
            pl.BlockSpec((1, d, d), lambda i: (layer, 0, 0)),
        ],
        out_specs=tok(d),
        out_shape=jax.ShapeDtypeStruct((m, d), F32),
        compiler_params=_cparams(("parallel",)),
        name="out_proj",
    )(h, o_nsa, o_lru, o_ml, gain, ind, indt, w_out)


def _pack_w_in(w_in):
    cols = np.concatenate([
        np.arange(_OFF_Q, _OFF_G),
        np.arange(_OFF_QM, _OFF_IM),
        np.arange(_OFF_OM, D_IN),
        np.arange(_OFF_XR, _OFF_QM),
        np.arange(_OFF_G, _OFF_XR),
        np.arange(_OFF_IM, _OFF_OM),
    ])
    main = w_in[:, :, cols]
    pad = LANES - (NSA_HEADS * 3 + 2 * ML_HEADS)
    main = jnp.pad(main, ((0, 0), (0, 0), (0, pad))).astype(BF16)
    t_cols = np.concatenate([np.arange(_OFF_KM, _OFF_VM), np.arange(_OFF_IM, _OFF_OM)])
    w_t = jnp.swapaxes(w_in[:, :, t_cols], 1, 2).astype(BF16)
    return main, w_t


def _block_diag(w):
    depth, n, c, _ = w.shape
    eye = jnp.eye(n, dtype=w.dtype)
    return jnp.einsum("lncd,nm->lncmd", w, eye).reshape(depth, n * c, n * c)


def _constants(s):
    n_cmp_pad = s // CMP_STRIDE
    n = np.arange(n_cmp_pad)
    j = np.arange(LANES)
    cmp_start = n * CMP_STRIDE
    cmp_end = cmp_start + CMP_BLOCK - 1
    slc_start = j * SEL_BLOCK
    n_cmp = (s - CMP_BLOCK) // CMP_STRIDE + 1
    ov = ((cmp_start[:, None] <= slc_start[None, :] + SEL_BLOCK - 1) & (cmp_end[:, None] >= slc_start[None, :])
          & (n[:, None] < n_cmp) & (j[None, :] < s // SEL_BLOCK))
    key_block = np.arange(s) // SEL_BLOCK
    ex = np.where(j[:, None] == key_block[None, :], NEG, 0.0)
    tri = np.triu(np.ones((ML_CHUNK, ML_CHUNK)))
    heads = np.arange(D_MODEL) // HEAD_DIM
    ind = (heads[:, None] == j[None, :])
    return (jnp.asarray(ov, BF16), jnp.asarray(ex, BF16), jnp.asarray(tri, BF16),
            jnp.asarray(ind, BF16), jnp.asarray(ind.T, BF16))


def _mixers(h3, layer, p):
    b, s, d = h3.shape
    heads, xr, gr, gates, kt, gt = _in_proj(h3, p["mix_norm"][layer][None], p["w_main"], p["w_t"], layer)

    heads_z = heads.reshape(b, N_SLOTS, s // CMP_STRIDE, CMP_STRIDE * HEAD_DIM)
    cmp_kv = _compress(heads_z, p["cmp_pos"][layer], p["cmp_w1"][layer], p["cmp_w2"][layer])
    o_nsa = _nsa(heads, cmp_kv, gates, p["ov"], p["ex"])

    o_lru = _rglru(xr, gr, p["conv_w"][layer], p["conv_b"][layer][None], p["wa"][layer], p["ba"][layer][None],
                   p["wx"][layer], p["bx"][layer][None], p["lam"][layer][None])

    nc = s // ML_CHUNK
    heads_c = heads.reshape(b, N_SLOTS, nc, ML_CHUNK, HEAD_DIM)
    gates_c = gates.reshape(b, nc, ML_CHUNK, LANES)
    o_ml = _mlstm(heads_c, kt, gates_c, gt, p["ml_bcol"][layer][None], p["ml_brow"][layer][:, None], p["tri"])
    m = b * s
    return o_nsa.reshape(m, NSA_WIDTH), o_lru.reshape(m, LRU_WIDTH), o_ml.reshape(m, ML_WIDTH)


def kernel(x, ffn1_norm, ffn1_w1, ffn1_w3, ffn1_w2, mix_norm, w_in, nsa_cmp_pos_k, nsa_cmp_w1_k, nsa_cmp_w2_k, nsa_cmp_pos_v, nsa_cmp_w1_v, nsa_cmp_w2_v, lru_conv_w, lru_conv_b, lru_w_a, lru_b_a, lru_w_x, lru_b_x, lru_lambda, ml_b_i, ml_b_f, head_norm, w_out, ffn2_norm, ffn2_w1, ffn2_w3, ffn2_w2, final_norm):
    b, s, d = x.shape
    m = b * s
    depth = w_in.shape[0]
    ov, ex, tri, ind, indt = _constants(s)
    w_main, w_t = _pack_w_in(w_in)
    flat = CMP_BLOCK * HEAD_DIM
    zeros_gate = jnp.zeros((depth, GATE_I), F32)
    bias_lanes = jnp.concatenate([zeros_gate, ml_b_i, ml_b_f], axis=1)
    p = {
        "mix_norm": mix_norm, "w_main": w_main, "w_t": w_t,
        "cmp_pos": jnp.stack([nsa_cmp_pos_k.reshape(depth, 1, flat), nsa_cmp_pos_v.reshape(depth, 1, flat)], axis=1),
        "cmp_w1": jnp.stack([nsa_cmp_w1_k, nsa_cmp_w1_v], axis=1).astype(BF16),
        "cmp_w2": jnp.stack([nsa_cmp_w2_k, nsa_cmp_w2_v], axis=1).astype(BF16),
        "ov": ov, "ex": ex, "tri": tri,
        "conv_w": lru_conv_w, "conv_b": lru_conv_b,
        "wa": _block_diag(lru_w_a).astype(BF16), "ba": lru_b_a.reshape(depth, LRU_WIDTH),
        "wx": _block_diag(lru_w_x).astype(BF16), "bx": lru_b_x.reshape(depth, LRU_WIDTH),
        "lam": lru_lambda,
        "ml_bcol": jnp.pad(bias_lanes, ((0, 0), (0, LANES - bias_lanes.shape[1]))),
        "ml_brow": jnp.concatenate([ml_b_i, ml_b_f], axis=1),
    }
    f1 = [w.astype(BF16) for w in (ffn1_w1, ffn1_w3, ffn1_w2)]
    f2 = [w.astype(BF16) for w in (ffn2_w1, ffn2_w3, ffn2_w2)]
    w_out_b = w_out.astype(BF16)
    gain_heads = head_norm.reshape(depth, 1, d)

    h = x.reshape(m, d)
    for layer in range(depth):
        h = _ffn(h, ffn1_norm[layer][None], *f1, layer)
        o_nsa, o_lru, o_ml = _mixers(h.reshape(b, s, d), layer, p)
        h = _out_proj(h, o_nsa, o_lru, o_ml, gain_heads[layer], ind, indt, w_out_b, layer)
        last = layer == depth - 1
        h = _ffn(h, ffn2_norm[layer][None], *f2, layer, final_norm[None] if last else None)
    return h.reshape(b, s, d)
```

```python
import functools

import jax
import jax.numpy as jnp
import numpy as np
from jax import lax
from jax.experimental import pallas as pl
from jax.experimental.pallas import tpu as pltpu

F32 = jnp.float32
BF16 = jnp.bfloat16

D_MODEL = 1024
DEPTH = 4
HEAD_DIM = 64
NSA_HEADS = 8
NSA_GROUPS = 2
NSA_HPG = NSA_HEADS // NSA_GROUPS
NSA_WIDTH = NSA_HEADS * HEAD_DIM
NSA_KV_WIDTH = NSA_GROUPS * HEAD_DIM
CMP_BLOCK = 32
CMP_STRIDE = 16
CMP_HIDDEN = 256
SEL_BLOCK = 64
SEL_TOPK = 16
SEL_LOCAL = 2
WINDOW = 512
FORCE_BONUS = 1.0e4
LRU_BLOCKS = 4
LRU_WIDTH = LRU_BLOCKS * HEAD_DIM
CONV_WIDTH = 4
LRU_C = 8.0
ML_HEADS = 4
ML_WIDTH = ML_HEADS * HEAD_DIM
ML_CHUNK = 64
D_FF = 2816
EPS = 1e-6
NEG = -1e30

LANES = 128
SUBLANES = 8
VMEM_LIMIT = 52 * 1024 * 1024

_OFF_Q = 0
_OFF_KV = NSA_WIDTH
_OFF_G = _OFF_KV + 6 * NSA_KV_WIDTH
_OFF_XR = _OFF_G + NSA_HEADS * 3
_OFF_GR = _OFF_XR + LRU_WIDTH
_OFF_QM = _OFF_GR + LRU_WIDTH
_OFF_KM = _OFF_QM + ML_WIDTH
_OFF_VM = _OFF_KM + ML_WIDTH
_OFF_IM = _OFF_VM + ML_WIDTH
_OFF_FM = _OFF_IM + ML_HEADS
_OFF_OM = _OFF_FM + ML_HEADS
D_IN = _OFF_OM + ML_WIDTH

N_SLOTS = 36
SLOT_Q, SLOT_KC, SLOT_VC, SLOT_KS, SLOT_VS, SLOT_KW, SLOT_VW = 0, 8, 10, 12, 14, 16, 18
SLOT_QM, SLOT_KM, SLOT_VM, SLOT_OM = 20, 24, 28, 32
GATE_I, GATE_F = NSA_HEADS * 3, NSA_HEADS * 3 + ML_HEADS
T_GATE = ML_WIDTH
T_NSAG = T_GATE + 2 * ML_HEADS
T_VS = T_NSAG + NSA_HEADS * 3
T_VW = T_VS + NSA_KV_WIDTH
N_T_ROWS = T_VW + NSA_KV_WIDTH
LOG2E = 1.4426950408889634


def _cparams(sem):
    return pltpu.CompilerParams(dimension_semantics=sem, vmem_limit_bytes=VMEM_LIMIT)


def _rms(x, g):
    return x * lax.rsqrt(jnp.mean(x * x, axis=-1, keepdims=True) + EPS) * g


def _split3(x):
    hi = x.astype(BF16)
    r1 = x - hi.astype(F32)
    mid = r1.astype(BF16)
    lo = (r1 - mid.astype(F32)).astype(BF16)
    return hi, mid, lo


def _dot_exact01(x, m01):
    hi, mid, lo = _split3(x)
    d = lambda a: jnp.dot(a, m01, preferred_element_type=F32)
    return d(hi) + d(mid) + d(lo)


def _ffn_body(h_ref, g_ref, w1_ref, w3_ref, w2_ref, fg_ref, o_ref, xn_ref, acc_ref, *, n_f, final):
    j = pl.program_id(1)

    @pl.when(j == 0)
    def _():
        xn_ref[...] = _rms(h_ref[...], g_ref[...]).astype(BF16)

    xn = xn_ref[...]
    a = jnp.dot(xn, w1_ref[0], preferred_element_type=F32)
    b = jnp.dot(xn, w3_ref[0], preferred_element_type=F32)
    gated = (a * jax.nn.sigmoid(a) * b).astype(BF16)
    part = jnp.dot(gated, w2_ref[0], preferred_element_type=F32)

    @pl.when(j == 0)
    def _():
        acc_ref[...] = part

    @pl.when(j > 0)
    def _():
        acc_ref[...] += part

    @pl.when(j == n_f - 1)
    def _():
        y = h_ref[...] + 0.5 * acc_ref[...]
        if final:
            y = _rms(y, fg_ref[...])
        o_ref[...] = y


def _ffn(h, gain, w1, w3, w2, layer, final_gain=None, *, tm=512, tf=1408):
    m, d = h.shape
    n_f = D_FF // tf
    final = final_gain is not None
    fg = final_gain if final else gain
    body = functools.partial(_ffn_body, n_f=n_f, final=final)
    return pl.pallas_call(
        body,
        grid=(m // tm, n_f),
        in_specs=[
            pl.BlockSpec((tm, d), lambda i, j: (i, 0)),
            pl.BlockSpec((1, d), lambda i, j: (0, 0)),
            pl.BlockSpec((1, d, tf), lambda i, j: (layer, 0, j)),
            pl.BlockSpec((1, d, tf), lambda i, j: (layer, 0, j)),
            pl.BlockSpec((1, tf, d), lambda i, j: (layer, j, 0)),
            pl.BlockSpec((1, d), lambda i, j: (0, 0)),
        ],
        out_specs=pl.BlockSpec((tm, d), lambda i, j: (i, 0)),
        out_shape=jax.ShapeDtypeStruct((m, d), F32),
        scratch_shapes=[pltpu.VMEM((tm, d), BF16), pltpu.VMEM((tm, d), F32)],
        compiler_params=_cparams(("parallel", "arbitrary")),
        name="ffn",
    )(h, gain, w1, w3, w2, fg)


def _inproj_body(h_ref, g_ref, w_ref, wt_ref, heads_ref, xr_ref, gr_ref, gates_ref, kt_ref, gt_ref, gn_ref, vt_ref,
                 *, tm):
    xn = _rms(h_ref[0], g_ref[...]).astype(BF16)
    per = 2 * LANES // HEAD_DIM
    for c in range(N_SLOTS // per):
        u = jnp.dot(xn, w_ref[0, :, c * 2 * LANES:(c + 1) * 2 * LANES], preferred_element_type=F32)
        for i in range(per):
            heads_ref[0, c * per + i] = u[:, i * HEAD_DIM:(i + 1) * HEAD_DIM]
    base = N_SLOTS * HEAD_DIM
    xr_ref[0] = jnp.dot(xn, w_ref[0, :, base:base + LRU_WIDTH], preferred_element_type=F32)
    gr_ref[0] = jnp.dot(xn, w_ref[0, :, base + LRU_WIDTH:base + 2 * LRU_WIDTH], preferred_element_type=F32)
    gates_ref[0] = jnp.dot(xn, w_ref[0, :, base + 2 * LRU_WIDTH:], preferred_element_type=F32)
    ut = lax.dot_general(wt_ref[0], xn, (((1,), (1,)), ((), ())), preferred_element_type=F32)
    for cc in range(tm // ML_CHUNK):
        cols = slice(cc * ML_CHUNK, (cc + 1) * ML_CHUNK)
        for hh in range(ML_HEADS):
            kt_ref[0, hh, cc] = ut[hh * HEAD_DIM:(hh + 1) * HEAD_DIM, cols]
        gt_ref[0, cc] = ut[T_GATE:T_NSAG, cols]
    gn_ref[0] = ut[T_NSAG:T_VS, :]
    for j in range(2 * NSA_GROUPS):
        vt_ref[0, j] = ut[T_VS + j * HEAD_DIM:T_VS + (j + 1) * HEAD_DIM, :]


def _in_proj(h3, gain, w_main, w_t, layer, *, tm=256):
    b, s, d = h3.shape
    n_main = w_main.shape[-1]
    nc = s // ML_CHUNK
    body = functools.partial(_inproj_body, tm=tm)
    return pl.pallas_call(
        body,
        grid=(b, s // tm),
        in_specs=[
            pl.BlockSpec((1, tm, d), lambda bi, ti: (bi, ti, 0)),
            pl.BlockSpec((1, d), lambda bi, ti: (0, 0)),
            pl.BlockSpec((1, d, n_main), lambda bi, ti: (layer, 0, 0)),
            pl.BlockSpec((1, N_T_ROWS, d), lambda bi, ti: (layer, 0, 0)),
        ],
        out_specs=[
            pl.BlockSpec((1, N_SLOTS, tm, HEAD_DIM), lambda bi, ti: (bi, 0, ti, 0)),
            pl.BlockSpec((1, tm, LRU_WIDTH), lambda bi, ti: (bi, ti, 0)),
            pl.BlockSpec((1, tm, LRU_WIDTH), lambda bi, ti: (bi, ti, 0)),
            pl.BlockSpec((1, tm, LANES), lambda bi, ti: (bi, ti, 0)),
            pl.BlockSpec((1, ML_HEADS, tm // ML_CHUNK, HEAD_DIM, ML_CHUNK), lambda bi, ti: (bi, 0, ti, 0, 0)),
            pl.BlockSpec((1, tm // ML_CHUNK, 2 * ML_HEADS, ML_CHUNK), lambda bi, ti: (bi, ti, 0, 0)),
            pl.BlockSpec((1, NSA_HEADS * 3, tm), lambda bi, ti: (bi, 0, ti)),
            pl.BlockSpec((1, 2 * NSA_GROUPS, HEAD_DIM, tm), lambda bi, ti: (bi, 0, 0, ti)),
        ],
        out_shape=[
            jax.ShapeDtypeStruct((b, N_SLOTS, s, HEAD_DIM), F32),
            jax.ShapeDtypeStruct((b, s, LRU_WIDTH), F32),
            jax.ShapeDtypeStruct((b, s, LRU_WIDTH), F32),
            jax.ShapeDtypeStruct((b, s, LANES), F32),
            jax.ShapeDtypeStruct((b, ML_HEADS, nc, HEAD_DIM, ML_CHUNK), F32),
            jax.ShapeDtypeStruct((b, nc, 2 * ML_HEADS, ML_CHUNK), F32),
            jax.ShapeDtypeStruct((b, NSA_HEADS * 3, s), F32),
            jax.ShapeDtypeStruct((b, 2 * NSA_GROUPS, HEAD_DIM, s), F32),
        ],
        compiler_params=_cparams(("parallel", "parallel")),
        name="in_proj",
    )(h3, gain, w_main, w_t)


def _compress_hidden(z_ref, pos_ref, w1_ref):
    n = z_ref.shape[2] // CMP_STRIDE
    top = bot = None
    for c in range(CMP_STRIDE):
        zc = z_ref[0, 0, pl.ds(c, n, stride=CMP_STRIDE), :]
        for half in range(2):
            off = half * CMP_STRIDE + c
            x = (zc + pos_ref[0, off:off + 1, :]).astype(BF16)
            part = jnp.dot(x, w1_ref[0, off * HEAD_DIM:(off + 1) * HEAD_DIM, :], preferred_element_type=F32)
            if half == 0:
                top = part if top is None else top + part
            else:
                bot = part if bot is None else bot + part
    hid = top + pltpu.roll(bot, n - 1, axis=0)
    return (hid * jax.nn.sigmoid(hid)).astype(BF16)


def _compress_body(zk_ref, zv_ref, pos_ref, w1_ref, w2k_ref, w2vt_ref, kc_ref, vct_ref):
    hk = _compress_hidden(zk_ref, pos_ref.at[0:1], w1_ref.at[0:1])
    hv = _compress_hidden(zv_ref, pos_ref.at[1:2], w1_ref.at[1:2])
    kc = jnp.dot(hk, w2k_ref[...], preferred_element_type=F32)
    n = kc.shape[0]
    lane = lax.broadcasted_iota(jnp.int32, (n, HEAD_DIM), 1)
    blk = lax.broadcasted_iota(jnp.int32, (n, HEAD_DIM), 0).astype(F32)
    idx = jnp.where(lane < 3, blk, 0.0)
    kc_ref[0, 0] = jnp.concatenate([kc, idx], axis=1).astype(BF16)
    vct = lax.dot_general(w2vt_ref[...], hv, (((1,), (1,)), ((), ())), preferred_element_type=F32)
    vct_ref[0, 0] = vct.astype(BF16)


def _compress(heads, pos, w1, w2k, w2vt):
    b, _, s, _ = heads.shape
    n = s // CMP_STRIDE
    slot = lambda first: pl.BlockSpec((1, 1, s, HEAD_DIM), lambda bi, g: (bi, first + g, 0, 0))
    whole = lambda a: pl.BlockSpec(a.shape, lambda bi, g: (0,) * a.ndim)
    return pl.pallas_call(
        _compress_body,
        grid=(b, NSA_GROUPS),
        in_specs=[slot(SLOT_KC), slot(SLOT_VC), whole(pos), whole(w1), whole(w2k), whole(w2vt)],
        out_specs=[
            pl.BlockSpec((1, 1, n, 2 * HEAD_DIM), lambda bi, g: (bi, g, 0, 0)),
            pl.BlockSpec((1, 1, HEAD_DIM, n), lambda bi, g: (bi, g, 0, 0)),
        ],
        out_shape=[
            jax.ShapeDtypeStruct((b, NSA_GROUPS, n, 2 * HEAD_DIM), BF16),
            jax.ShapeDtypeStruct((b, NSA_GROUPS, HEAD_DIM, n), BF16),
        ],
        compiler_params=_cparams(("parallel", "parallel")),
        name="compress",
    )(heads, heads, pos, w1, w2k, w2vt)


TQ = 128
N_SEL = 32


N_IDX = 3


def _nsa_body(q_ref, sl_ref, slrow_ref, kc_ref, vct_ref, ks_ref, kw_ref, vst_ref, vwt_ref, gn_ref, ovt_ref, ext_ref,
              o_ref, ksa_sc, kwa_sc, vs_sc, vw_sc):
    g = pl.program_id(1)
    qi = pl.program_id(2)
    s_len = ks_ref.shape[2]
    r_cols = NSA_HPG * TQ
    nt = (((1,), (1,)), ((), ()))

    @pl.when(qi == 0)
    def _():
        lane = lax.broadcasted_iota(jnp.int32, (TQ, HEAD_DIM), 1)
        loc = lax.broadcasted_iota(jnp.int32, (TQ, HEAD_DIM), 0).astype(F32)
        idx = jnp.where(lane < N_IDX, loc, 0.0)
        for c in range(s_len // TQ):
            rows = slice(c * TQ, (c + 1) * TQ)
            ksa_sc[rows, :] = jnp.concatenate([ks_ref[0, 0, rows, :], idx], axis=1).astype(BF16)
            kwa_sc[rows, :] = jnp.concatenate([kw_ref[0, 0, rows, :], idx], axis=1).astype(BF16)
        vs_sc[...] = vst_ref[0, 0].astype(BF16)
        vw_sc[...] = vwt_ref[0, 0].astype(BF16)

    q = q_ref[0].reshape(r_cols, HEAD_DIM) * (HEAD_DIM ** -0.5 * LOG2E)
    q_tile = jnp.concatenate([q, sl_ref[0, 0]], axis=1).astype(BF16)
    q_cmp = jnp.concatenate([q, sl_ref[0, 1]], axis=1).astype(BF16)
    sl_row = slrow_ref[0]

    def tile4(x):
        return jnp.concatenate([x] * NSA_HPG, axis=1)

    k_loc = lax.broadcasted_iota(jnp.int32, (TQ, TQ), 0)
    q_loc = lax.broadcasted_iota(jnp.int32, (TQ, TQ), 1)

    s_c = lax.dot_general(kc_ref[0, 0], q_cmp, nt, preferred_element_type=F32)
    ok_c = tile4(k_loc * CMP_STRIDE + (CMP_BLOCK - 1) <= qi * TQ + q_loc)
    s_c = jnp.where(ok_c, s_c, NEG)
    m_c = jnp.max(s_c, axis=0, keepdims=True)
    e_c = jnp.where(ok_c, jnp.exp2(s_c - m_c), 0.0)
    l_c = jnp.sum(e_c, axis=0, keepdims=True)
    p_c = e_c / jnp.where(l_c > 0.0, l_c, 1.0)
    o_c = jnp.dot(vct_ref[0, 0], p_c.astype(BF16), preferred_element_type=F32)

    p_sum = p_c[:, 0:TQ]
    for r in range(1, NSA_HPG):
        p_sum = p_sum + p_c[:, r * TQ:(r + 1) * TQ]
    ovt = ovt_ref[...]
    imp = sum(jnp.dot(ovt, part, preferred_element_type=F32) for part in _split3(p_sum))[:N_SEL]
    blk = lax.broadcasted_iota(jnp.int32, (N_SEL, TQ), 0)
    cur = jnp.right_shift(qi * TQ + lax.broadcasted_iota(jnp.int32, (N_SEL, TQ), 1), int(np.log2(SEL_BLOCK)))
    valid = blk <= cur
    forced = valid & ((blk == 0) | (blk > cur - SEL_LOCAL))
    score = jnp.where(valid, imp + jnp.where(forced, FORCE_BONUS, 0.0), NEG)
    rank = jnp.zeros((N_SEL, TQ), F32)
    for jp in range(N_SEL):
        row = score[jp:jp + 1, :]
        beats = (row > score) | ((row == score) & (blk > jp))
        rank = rank + jnp.where(beats, 1.0, 0.0)
    unsel = jnp.where(rank < float(SEL_TOPK), 0.0, 1.0)
    unsel = jnp.concatenate([unsel, jnp.zeros((LANES - N_SEL, TQ), F32)], axis=0).astype(BF16)

    def attend(ka_sc, vt_sc, kt, carry, extra=None, keep=None):
        m, l, acc = carry
        start = pl.multiple_of(kt * TQ, TQ)
        s = lax.dot_general(ka_sc[pl.ds(start, TQ), :], q_tile, nt, preferred_element_type=F32)
        if extra is not None:
            s = s + tile4(extra(start))
        if keep is not None:
            s = jnp.where(tile4(keep), s, NEG)
        shift = start.astype(F32) * sl_row
        m_new = jnp.maximum(m, jnp.max(s, axis=0, keepdims=True) + shift)
        p = jnp.exp2(s - (m_new - shift))
        alpha = jnp.exp2(m - m_new)
        l = alpha * l + jnp.sum(p, axis=0, keepdims=True)
        pv = jnp.dot(vt_sc[:, pl.ds(start, TQ)], p.astype(BF16), preferred_element_type=F32)
        return m_new, l, alpha * acc + pv

    init = (jnp.full((1, r_cols), NEG, F32), jnp.zeros((1, r_cols), F32), jnp.zeros((HEAD_DIM, r_cols), F32))
    causal = k_loc <= q_loc

    def sel_bias(start):
        return jnp.dot(ext_ref[pl.ds(start, TQ), :], unsel, preferred_element_type=F32)

    carry = lax.fori_loop(0, qi, lambda kt, c: attend(ksa_sc, vs_sc, kt, c, extra=sel_bias), init)
    _, l_s, acc_s = attend(ksa_sc, vs_sc, qi, carry, extra=sel_bias, keep=causal)
    o_s = acc_s / l_s

    n_back = WINDOW // TQ
    carry = lax.cond(qi >= n_back,
                     lambda c: attend(kwa_sc, vw_sc, qi - n_back, c, keep=k_loc > q_loc),
                     lambda c: c, init)
    carry = lax.fori_loop(jnp.maximum(qi - n_back + 1, 0), qi, lambda kt, c: attend(kwa_sc, vw_sc, kt, c), carry)
    _, l_w, acc_w = attend(kwa_sc, vw_sc, qi, carry, keep=causal)
    o_w = acc_w / l_w

    gates = jax.nn.sigmoid(gn_ref[0])
    outs = []
    for r in range(NSA_HPG):
        cols = slice(r * TQ, (r + 1) * TQ)
        acc = None
        for br, o_b in enumerate((o_c, o_s, o_w)):
            row_lo = r * 3 + br
            g0 = gates[row_lo:row_lo + 1, :]
            g1 = gates[NSA_HPG * 3 + row_lo:NSA_HPG * 3 + row_lo + 1, :]
            term = jnp.where(g == 0, g0, g1) * o_b[:, cols]
            acc = term if acc is None else acc + term
        outs.append(acc)
    o_ref[0] = jnp.concatenate(outs, axis=0).T


def _nsa(heads, kc_aug, vct, vt, gn, sl, slrow, ovt, ext):
    b, _, s, _ = heads.shape
    n_cmp = kc_aug.shape[2]
    assert n_cmp == TQ and s // SEL_BLOCK == N_SEL
    r_cols = NSA_HPG * TQ
    k_spec = lambda slot: pl.BlockSpec((1, 1, s, HEAD_DIM), lambda bi, g, qi: (bi, slot + g, 0, 0))
    vt_spec = lambda first: pl.BlockSpec((1, 1, HEAD_DIM, s), lambda bi, g, qi: (bi, first + g, 0, 0))
    return pl.pallas_call(
        _nsa_body,
        grid=(b, NSA_GROUPS, s // TQ),
        in_specs=[
            pl.BlockSpec((1, NSA_HPG, TQ, HEAD_DIM), lambda bi, g, qi: (bi, g, qi, 0)),
            pl.BlockSpec((1, 2, r_cols, HEAD_DIM), lambda bi, g, qi: (g, 0, 0, 0)),
            pl.BlockSpec((1, 1, r_cols), lambda bi, g, qi: (g, 0, 0)),
            pl.BlockSpec((1, 1, n_cmp, 2 * HEAD_DIM), lambda bi, g, qi: (bi, g, 0, 0)),
            pl.BlockSpec((1, 1, HEAD_DIM, n_cmp), lambda bi, g, qi: (bi, g, 0, 0)),
            k_spec(SLOT_KS), k_spec(SLOT_KW), vt_spec(0), vt_spec(NSA_GROUPS),
            pl.BlockSpec((1, NSA_HEADS * 3, TQ), lambda bi, g, qi: (bi, 0, qi)),
            pl.BlockSpec((LANES, LANES), lambda bi, g, qi: (0, 0)),
            pl.BlockSpec((s, LANES), lambda bi, g, qi: (0, 0)),
        ],
        out_specs=pl.BlockSpec((1, TQ, NSA_HPG * HEAD_DIM), lambda bi, g, qi: (bi, qi, g)),
        out_shape=jax.ShapeDtypeStruct((b, s, NSA_WIDTH), F32),
        scratch_shapes=[
            pltpu.VMEM((s, 2 * HEAD_DIM), BF16),
            pltpu.VMEM((s, 2 * HEAD_DIM), BF16),
            pltpu.VMEM((HEAD_DIM, s), BF16),
            pltpu.VMEM((HEAD_DIM, s), BF16),
        ],
        compiler_params=_cparams(("arbitrary", "arbitrary", "arbitrary")),
        name="nsa",
    )(heads, sl, slrow, kc_aug, vct, heads, heads, vt, vt, gn, ovt, ext)


SCAN_PAD = 1024
SCAN_ROWS = 256


def _lru_body(x_ref, g_ref, cw_ref, cb_ref, wa_ref, ba_ref, wx_ref, bx_ref, lam_ref, o_ref,
              a0, b0, a1, b1):
    s = x_ref.shape[1]
    x = x_ref[0]
    row = lax.broadcasted_iota(jnp.int32, x.shape, 0)
    xc = cb_ref[...]
    for tap in range(CONV_WIDTH):
        back = CONV_WIDTH - 1 - tap
        xs = x if back == 0 else jnp.where(row >= back, pltpu.roll(x, back, axis=0), 0.0)
        xc = xc + xs * cw_ref[tap:tap + 1, :]
    xb = xc.astype(BF16)
    r = jax.nn.sigmoid(jnp.dot(xb, wa_ref[...], preferred_element_type=F32) + ba_ref[...])
    i = jax.nn.sigmoid(jnp.dot(xb, wx_ref[...], preferred_element_type=F32) + bx_ref[...])
    log_a = -LRU_C * r * jax.nn.softplus(-lam_ref[...])
    a = jnp.exp(log_a)
    u = jnp.sqrt(-jnp.tanh(log_a) * (a * a + 1.0)) * (i * xc)

    bufs = ((a0, b0), (a1, b1))
    for ab, bb in bufs:
        ab[0:SCAN_PAD, :] = jnp.ones((SCAN_PAD, x.shape[1]), F32)
        bb[0:SCAN_PAD, :] = jnp.zeros((SCAN_PAD, x.shape[1]), F32)
    a0[SCAN_PAD:, :] = a
    b0[SCAN_PAD:, :] = u
    n_steps = int(np.log2(s))
    for k in range(n_steps):
        d = 1 << k
        (sa, sb), (da, db) = bufs[k % 2], bufs[(k + 1) % 2]
        for c in range(s // SCAN_ROWS):
            lo = SCAN_PAD + c * SCAN_ROWS
            a_cur = sa[lo:lo + SCAN_ROWS, :]
            a_prev = sa[lo - d:lo - d + SCAN_ROWS, :]
            b_cur = sb[lo:lo + SCAN_ROWS, :]
            b_prev = sb[lo - d:lo - d + SCAN_ROWS, :]
            da[lo:lo + SCAN_ROWS, :] = a_cur * a_prev
            db[lo:lo + SCAN_ROWS, :] = a_cur * b_prev + b_cur
    h = bufs[n_steps % 2][1][SCAN_PAD:, :]
    o_ref[0] = h * jax.nn.gelu(g_ref[0])


def _rglru(xr, gr, conv_w, conv_b, wa, ba, wx, bx, lam):
    b, s, c = xr.shape
    assert s <= 2 * SCAN_PAD and s % SCAN_ROWS == 0
    tok = pl.BlockSpec((1, s, c), lambda bi: (bi, 0, 0))
    vec = lambda n: pl.BlockSpec((n, c), lambda bi: (0, 0))
    return pl.pallas_call(
        _lru_body,
        grid=(b,),
        in_specs=[tok, tok, vec(CONV_WIDTH), vec(1), vec(c), vec(1), vec(c), vec(1), vec(1)],
        out_specs=tok,
        out_shape=jax.ShapeDtypeStruct((b, s, c), F32),
        scratch_shapes=[pltpu.VMEM((SCAN_PAD + s, c), F32)] * 4,
        compiler_params=_cparams(("parallel",)),
        name="rglru",
    )(xr, gr, conv_w, conv_b, wa, ba, wx, bx, lam)


def _mlstm_body(q_ref, v_ref, o_ref_in, kt_ref, gcol_ref, grow_ref, bcol_ref, brow_ref, tri_ref, out_ref):
    nc = q_ref.shape[2]
    L = ML_CHUNK
    ones_col = jnp.where(lax.broadcasted_iota(jnp.int32, (L, HEAD_DIM), 1) == 0, 1.0, 0.0)
    rr = lax.broadcasted_iota(jnp.int32, (L, L), 0)
    cc = lax.broadcasted_iota(jnp.int32, (L, L), 1)
    causal = cc <= rr
    srow = lax.broadcasted_iota(jnp.int32, (L, LANES), 0)

    def chunk(c, carry):
        gcol = gcol_ref[0, c] + bcol_ref[...]
        lf_col = jax.nn.log_sigmoid(gcol)
        a_col = lf_col
        for k in range(int(np.log2(L))):
            d = 1 << k
            a_col = a_col + jnp.where(srow >= d, pltpu.roll(a_col, d, axis=0), 0.0)
        grow = grow_ref[0, c] + brow_ref[...]
        lf_row = jax.nn.log_sigmoid(grow)
        a_rows = _dot_exact01(lf_row, tri_ref[...])
        new_carry = []
        outs = []
        for h in range(ML_HEADS):
            c_aug, m_prev = carry[h]
            q = q_ref[0, h, c].astype(BF16)
            kt = kt_ref[0, h, c]
            v = v_ref[0, h, c]
            v_aug = jnp.concatenate([v, ones_col], axis=1).astype(BF16)
            a_c = a_col[:, GATE_F + h:GATE_F + h + 1]
            a_r = jnp.broadcast_to(a_rows[ML_HEADS + h:ML_HEADS + h + 1, :], (SUBLANES, L))
            ig_r = jnp.broadcast_to(grow[h:h + 1, :], (SUBLANES, L))
            a_end = a_r[:, L - 1:L]
            tall = lambda s8: jnp.concatenate([s8] * (L // SUBLANES), axis=0)

            qk = jnp.dot(q, kt.astype(BF16), preferred_element_type=F32) * HEAD_DIM ** -0.5
            dmat = jnp.where(causal, a_c + (ig_r - a_r)[0:1, :], NEG)
            m_inter = a_c + tall(m_prev)
            m = jnp.maximum(m_inter, jnp.max(dmat, axis=1, keepdims=True))
            inter = jnp.exp(m_inter - m)
            w = jnp.exp(dmat - m) * qk
            num = inter * jnp.dot(q, c_aug.astype(BF16), preferred_element_type=F32)
            num = num + jnp.dot(w.astype(BF16), v_aug, preferred_element_type=F32)
            den = num[:, HEAD_DIM:HEAD_DIM + 1]
            hval = num[:, :HEAD_DIM] / jnp.maximum(jnp.abs(den), jnp.exp(-m))
            outs.append(jax.nn.sigmoid(o_ref_in[0, h, c]) * hval)

            w_end = a_end - a_r + ig_r
            m_new = jnp.maximum(a_end + m_prev, jnp.max(w_end, axis=1, keepdims=True))
            decay = jnp.exp(a_end + m_prev - m_new)
            wk = jnp.exp(w_end - m_new) * HEAD_DIM ** -0.5
            kw = (kt * wk[0:1, :]).astype(BF16)
            c_new = tall(decay) * c_aug + jnp.dot(kw, v_aug, preferred_element_type=F32)
            new_carry.append((c_new, m_new))
        out_ref[0, c] = jnp.concatenate(outs, axis=1)
        return tuple(new_carry)

    init = tuple((jnp.zeros((HEAD_DIM, 2 * HEAD_DIM), F32), jnp.zeros((SUBLANES, 1), F32)) for _ in range(ML_HEADS))
    lax.fori_loop(0, nc, chunk, init)


def _mlstm(heads_c, kt, gates_c, gt, bcol, brow, tri):
    b, _, nc, L, _ = heads_c.shape
    slot_spec = lambda slot: pl.BlockSpec((1, ML_HEADS, nc, L, HEAD_DIM), lambda bi: (bi, slot // ML_HEADS, 0, 0, 0))
    return pl.pallas_call(
        _mlstm_body,
        grid=(b,),
        in_specs=[
            slot_spec(SLOT_QM), slot_spec(SLOT_VM), slot_spec(SLOT_OM),
            pl.BlockSpec((1, ML_HEADS, nc, HEAD_DIM, L), lambda bi: (bi, 0, 0, 0, 0)),
            pl.BlockSpec((1, nc, L, LANES), lambda bi: (bi, 0, 0, 0)),
            pl.BlockSpec((1, nc, 2 * ML_HEADS, L), lambda bi: (bi, 0, 0, 0)),
            pl.BlockSpec((1, LANES), lambda bi: (0, 0)),
            pl.BlockSpec((2 * ML_HEADS, 1), lambda bi: (0, 0)),
            pl.BlockSpec((L, L), lambda bi: (0, 0)),
        ],
        out_specs=pl.BlockSpec((1, nc, L, ML_WIDTH), lambda bi: (bi, 0, 0, 0)),
        out_shape=jax.ShapeDtypeStruct((b, nc, L, ML_WIDTH), F32),
        compiler_params=_cparams(("parallel",)),
        name="mlstm",
    )(heads_c, heads_c, heads_c, kt, gates_c, gt, bcol, brow, tri)


def _outproj_body(h_ref, on_ref, ol_ref, om_ref, gn_ref, ind_ref, indt_ref, w_ref, o_ref):
    acc = h_ref[...]
    col = 0
    for x_ref in (on_ref, ol_ref, om_ref):
        x = x_ref[...]
        width = x.shape[1]
        ind = ind_ref[col:col + width, :]
        ms = _dot_exact01(x * x, ind) * (1.0 / HEAD_DIM)
        scale = _dot_exact01(lax.rsqrt(ms + EPS), indt_ref[:, col:col + width])
        xn = (x * scale * gn_ref[:, col:col + width]).astype(BF16)
        acc = acc + jnp.dot(xn, w_ref[0, col:col + width, :], preferred_element_type=F32)
        col += width
    o_ref[...] = acc


def _out_proj(h, o_nsa, o_lru, o_ml, gain, ind, indt, w_out, layer, *, tm=512):
    m, d = h.shape
    tok = lambda w: pl.BlockSpec((tm, w), lambda i: (i, 0))
    return pl.pallas_call(
        _outproj_body,
        grid=(m // tm,),
        in_specs=[
            tok(d), tok(o_nsa.shape[1]), tok(o_lru.shape[1]), tok(o_ml.shape[1]),
            pl.BlockSpec((1, d), lambda i: (0, 0)),
            pl.BlockSpec((d, LANES), lambda i: (0, 0)),
            pl.BlockSpec((LANES, d), lambda i: (0, 0)),
            pl.BlockSpec((1, d, d), lambda i: (layer, 0, 0)),
        ],
        out_specs=tok(d),
        out_shape=jax.ShapeDtypeStruct((m, d), F32),
        compiler_params=_cparams(("parallel",)),
        name="out_proj",
    )(h, o_nsa, o_lru, o_ml, gain, ind, indt, w_out)


def _pack_w_in(w_in):
    cols = np.concatenate([
        np.arange(_OFF_Q, _OFF_G),
        np.arange(_OFF_QM, _OFF_IM),
        np.arange(_OFF_OM, D_IN),
        np.arange(_OFF_XR, _OFF_QM),
        np.arange(_OFF_G, _OFF_XR),
        np.arange(_OFF_IM, _OFF_OM),
    ])
    main = w_in[:, :, cols]
    pad = LANES - (NSA_HEADS * 3 + 2 * ML_HEADS)
    main = jnp.pad(main, ((0, 0), (0, 0), (0, pad))).astype(BF16)
    off_vs = _OFF_KV + 3 * NSA_KV_WIDTH
    off_vw = _OFF_KV + 5 * NSA_KV_WIDTH
    t_cols = np.concatenate([np.arange(_OFF_KM, _OFF_VM), np.arange(_OFF_IM, _OFF_OM), np.arange(_OFF_G, _OFF_XR),
                             np.arange(off_vs, off_vs + NSA_KV_WIDTH), np.arange(off_vw, off_vw + NSA_KV_WIDTH)])
    w_t = jnp.swapaxes(w_in[:, :, t_cols], 1, 2).astype(BF16)
    return main, w_t


def _block_diag(w):
    depth, n, c, _ = w.shape
    eye = jnp.eye(n, dtype=w.dtype)
    return jnp.einsum("lncd,nm->lncmd", w, eye).reshape(depth, n * c, n * c)


def _constants(s):
    n_cmp_pad = s // CMP_STRIDE
    n = np.arange(n_cmp_pad)
    j = np.arange(LANES)
    cmp_start = n * CMP_STRIDE
    cmp_end = cmp_start + CMP_BLOCK - 1
    slc_start = j * SEL_BLOCK
    n_cmp = (s - CMP_BLOCK) // CMP_STRIDE + 1
    ov = ((cmp_start[:, None] <= slc_start[None, :] + SEL_BLOCK - 1) & (cmp_end[:, None] >= slc_start[None, :])
          & (n[:, None] < n_cmp) & (j[None, :] < s // SEL_BLOCK))
    key_block = np.arange(s) // SEL_BLOCK
    ext = np.where(key_block[:, None] == j[None, :], NEG, 0.0)
    tri = np.triu(np.ones((ML_CHUNK, ML_CHUNK)))
    heads = np.arange(D_MODEL) // HEAD_DIM
    ind = (heads[:, None] == j[None, :])
    slope = 2.0 ** (-8.0 * np.arange(1, NSA_HEADS + 1) / NSA_HEADS) * LOG2E
    parts = jnp.stack(_split3(jnp.asarray(slope, F32)), axis=1).astype(F32)
    cols = jnp.pad(parts, ((0, 0), (0, HEAD_DIM - N_IDX)))
    cols = jnp.repeat(cols, TQ, axis=0).reshape(NSA_GROUPS, 1, NSA_HPG * TQ, HEAD_DIM)
    sl = jnp.concatenate([cols, cols * float(CMP_STRIDE)], axis=1)
    slrow = jnp.repeat(jnp.asarray(slope, F32), TQ).reshape(NSA_GROUPS, 1, NSA_HPG * TQ)
    return dict(ovt=jnp.asarray(ov.T, BF16), ext=jnp.asarray(ext, BF16), tri=jnp.asarray(tri, BF16),
                ind=jnp.asarray(ind, BF16), indt=jnp.asarray(ind.T, BF16), sl=sl, slrow=slrow)


def _mixers(h3, layer, p):
    b, s, d = h3.shape
    heads, xr, gr, gates, kt, gt, gn, vt = _in_proj(h3, p["mix_norm"][layer][None], p["w_main"], p["w_t"], layer)

    kc_aug, vct = _compress(heads, p["cmp_pos"][layer], p["cmp_w1"][layer], p["cmp_w2k"][layer], p["cmp_w2vt"][layer])
    o_nsa = _nsa(heads, kc_aug, vct, vt, gn, p["sl"], p["slrow"], p["ovt"], p["ext"])

    o_lru = _rglru(xr, gr, p["conv_w"][layer], p["conv_b"][layer][None], p["wa"][layer], p["ba"][layer][None],
                   p["wx"][layer], p["bx"][layer][None], p["lam"][layer][None])

    nc = s // ML_CHUNK
    heads_c = heads.reshape(b, N_SLOTS, nc, ML_CHUNK, HEAD_DIM)
    gates_c = gates.reshape(b, nc, ML_CHUNK, LANES)
    o_ml = _mlstm(heads_c, kt, gates_c, gt, p["ml_bcol"][layer][None], p["ml_brow"][layer][:, None], p["tri"])
    m = b * s
    return o_nsa.reshape(m, NSA_WIDTH), o_lru.reshape(m, LRU_WIDTH), o_ml.reshape(m, ML_WIDTH)


def kernel(x, ffn1_norm, ffn1_w1, ffn1_w3, ffn1_w2, mix_norm, w_in, nsa_cmp_pos_k, nsa_cmp_w1_k, nsa_cmp_w2_k, nsa_cmp_pos_v, nsa_cmp_w1_v, nsa_cmp_w2_v, lru_conv_w, lru_conv_b, lru_w_a, lru_b_a, lru_w_x, lru_b_x, lru_lambda, ml_b_i, ml_b_f, head_norm, w_out, ffn2_norm, ffn2_w1, ffn2_w3, ffn2_w2, final_norm):
    b, s, d = x.shape
    m = b * s
    depth = w_in.shape[0]
    consts = _constants(s)
    ind, indt = consts["ind"], consts["indt"]
    w_main, w_t = _pack_w_in(w_in)
    zeros_gate = jnp.zeros((depth, GATE_I), F32)
    bias_lanes = jnp.concatenate([zeros_gate, ml_b_i, ml_b_f], axis=1)
    p = {
        **consts,
        "mix_norm": mix_norm, "w_main": w_main, "w_t": w_t,
        "cmp_pos": jnp.stack([nsa_cmp_pos_k, nsa_cmp_pos_v], axis=1),
        "cmp_w1": jnp.stack([nsa_cmp_w1_k, nsa_cmp_w1_v], axis=1).astype(BF16),
        "cmp_w2k": nsa_cmp_w2_k.astype(BF16),
        "cmp_w2vt": jnp.swapaxes(nsa_cmp_w2_v, 1, 2).astype(BF16),
        "conv_w": lru_conv_w, "conv_b": lru_conv_b,
        "wa": _block_diag(lru_w_a).astype(BF16), "ba": lru_b_a.reshape(depth, LRU_WIDTH),
        "wx": _block_diag(lru_w_x).astype(BF16), "bx": lru_b_x.reshape(depth, LRU_WIDTH),
        "lam": lru_lambda,
        "ml_bcol": jnp.pad(bias_lanes, ((0, 0), (0, LANES - bias_lanes.shape[1]))),
        "ml_brow": jnp.concatenate([ml_b_i, ml_b_f], axis=1),
    }
    f1 = [w.astype(BF16) for w in (ffn1_w1, ffn1_w3, ffn1_w2)]
    f2 = [w.astype(BF16) for w in (ffn2_w1, ffn2_w3, ffn2_w2)]
    w_out_b = w_out.astype(BF16)
    gain_heads = head_norm.reshape(depth, 1, d)

    h = x.reshape(m, d)
    for layer in range(depth):
        h = _ffn(h, ffn1_norm[layer][None], *f1, layer)
        o_nsa, o_lru, o_ml = _mixers(h.reshape(b, s, d), layer, p)
        h = _out_proj(h, o_nsa, o_lru, o_ml, gain_heads[layer], ind, indt, w_out_b, layer)
        last = layer == depth - 1
        h = _ffn(h, ffn2_norm[layer][None], *f2, layer, final_norm[None] if last else None)
    return h.reshape(b, s, d)
```

```python
import functools

import jax
import jax.numpy as jnp
import numpy as np
from jax import lax
from jax.experimental import pallas as pl
from jax.experimental.pallas import tpu as pltpu

F32 = jnp.float32
BF16 = jnp.bfloat16

D_MODEL = 1024
DEPTH = 4
HEAD_DIM = 64
NSA_HEADS = 8
NSA_GROUPS = 2
NSA_HPG = NSA_HEADS // NSA_GROUPS
NSA_WIDTH = NSA_HEADS * HEAD_DIM
NSA_KV_WIDTH = NSA_GROUPS * HEAD_DIM
CMP_BLOCK = 32
CMP_STRIDE = 16
CMP_HIDDEN = 256
SEL_BLOCK = 64
SEL_TOPK = 16
SEL_LOCAL = 2
WINDOW = 512
FORCE_BONUS = 1.0e4
LRU_BLOCKS = 4
LRU_WIDTH = LRU_BLOCKS * HEAD_DIM
CONV_WIDTH = 4
LRU_C = 8.0
ML_HEADS = 4
ML_WIDTH = ML_HEADS * HEAD_DIM
ML_CHUNK = 64
D_FF = 2816
EPS = 1e-6
NEG = -1e30

LANES = 128
SUBLANES = 8
VMEM_LIMIT = 52 * 1024 * 1024

_OFF_Q = 0
_OFF_KV = NSA_WIDTH
_OFF_G = _OFF_KV + 6 * NSA_KV_WIDTH
_OFF_XR = _OFF_G + NSA_HEADS * 3
_OFF_GR = _OFF_XR + LRU_WIDTH
_OFF_QM = _OFF_GR + LRU_WIDTH
_OFF_KM = _OFF_QM + ML_WIDTH
_OFF_VM = _OFF_KM + ML_WIDTH
_OFF_IM = _OFF_VM + ML_WIDTH
_OFF_FM = _OFF_IM + ML_HEADS
_OFF_OM = _OFF_FM + ML_HEADS
D_IN = _OFF_OM + ML_WIDTH

N_SLOTS = 36
SLOT_Q, SLOT_KC, SLOT_VC, SLOT_KS, SLOT_VS, SLOT_KW, SLOT_VW = 0, 8, 10, 12, 14, 16, 18
SLOT_QM, SLOT_KM, SLOT_VM, SLOT_OM = 20, 24, 28, 32
GATE_I, GATE_F = NSA_HEADS * 3, NSA_HEADS * 3 + ML_HEADS
T_GATE = ML_WIDTH
T_NSAG = T_GATE + 2 * ML_HEADS
T_VS = T_NSAG + NSA_HEADS * 3
T_VW = T_VS + NSA_KV_WIDTH
N_T_ROWS = T_VW + NSA_KV_WIDTH
LOG2E = 1.4426950408889634


def _cparams(sem):
    return pltpu.CompilerParams(dimension_semantics=sem, vmem_limit_bytes=VMEM_LIMIT)


def _rms(x, g):
    return x * lax.rsqrt(jnp.mean(x * x, axis=-1, keepdims=True) + EPS) * g


def _split3(x):
    hi = x.astype(BF16)
    r1 = x - hi.astype(F32)
    mid = r1.astype(BF16)
    lo = (r1 - mid.astype(F32)).astype(BF16)
    return hi, mid, lo


def _dot_exact01(x, m01):
    hi, mid, lo = _split3(x)
    d = lambda a: jnp.dot(a, m01, preferred_element_type=F32)
    return d(hi) + d(mid) + d(lo)


def _ffn_body(h_ref, g_ref, w1_ref, w3_ref, w2_ref, fg_ref, o_ref, xn_ref, acc_ref, *, n_f, final):
    j = pl.program_id(1)

    @pl.when(j == 0)
    def _():
        xn_ref[...] = _rms(h_ref[...], g_ref[...]).astype(BF16)

    xn = xn_ref[...]
    a = jnp.dot(xn, w1_ref[0], preferred_element_type=F32)
    b = jnp.dot(xn, w3_ref[0], preferred_element_type=F32)
    gated = (a * jax.nn.sigmoid(a) * b).astype(BF16)
    part = jnp.dot(gated, w2_ref[0], preferred_element_type=F32)

    @pl.when(j == 0)
    def _():
        acc_ref[...] = part

    @pl.when(j > 0)
    def _():
        acc_ref[...] += part

    @pl.when(j == n_f - 1)
    def _():
        y = h_ref[...] + 0.5 * acc_ref[...]
        if final:
            y = _rms(y, fg_ref[...])
        o_ref[...] = y


def _ffn(h, gain, w1, w3, w2, layer, final_gain=None, *, tm=512, tf=1408):
    m, d = h.shape
    n_f = D_FF // tf
    final = final_gain is not None
    fg = final_gain if final else gain
    body = functools.partial(_ffn_body, n_f=n_f, final=final)
    return pl.pallas_call(
        body,
        grid=(m // tm, n_f),
        in_specs=[
            pl.BlockSpec((tm, d), lambda i, j: (i, 0)),
            pl.BlockSpec((1, d), lambda i, j: (0, 0)),
            pl.BlockSpec((1, d, tf), lambda i, j: (layer, 0, j)),
            pl.BlockSpec((1, d, tf), lambda i, j: (layer, 0, j)),
            pl.BlockSpec((1, tf, d), lambda i, j: (layer, j, 0)),
            pl.BlockSpec((1, d), lambda i, j: (0, 0)),
        ],
        out_specs=pl.BlockSpec((tm, d), lambda i, j: (i, 0)),
        out_shape=jax.ShapeDtypeStruct((m, d), F32),
        scratch_shapes=[pltpu.VMEM((tm, d), BF16), pltpu.VMEM((tm, d), F32)],
        compiler_params=_cparams(("parallel", "arbitrary")),
        name="ffn",
    )(h, gain, w1, w3, w2, fg)


def _inproj_body(h_ref, g_ref, w_ref, wt_ref, heads_ref, xr_ref, gr_ref, gates_ref, kt_ref, gt_ref, gn_ref, vt_ref,
                 *, tm):
    xn = _rms(h_ref[0], g_ref[...]).astype(BF16)
    per = 2 * LANES // HEAD_DIM
    for c in range(N_SLOTS // per):
        u = jnp.dot(xn, w_ref[0, :, c * 2 * LANES:(c + 1) * 2 * LANES], preferred_element_type=F32)
        for i in range(per):
            heads_ref[0, c * per + i] = u[:, i * HEAD_DIM:(i + 1) * HEAD_DIM]
    base = N_SLOTS * HEAD_DIM
    xr_ref[0] = jnp.dot(xn, w_ref[0, :, base:base + LRU_WIDTH], preferred_element_type=F32)
    gr_ref[0] = jnp.dot(xn, w_ref[0, :, base + LRU_WIDTH:base + 2 * LRU_WIDTH], preferred_element_type=F32)
    gates_ref[0] = jnp.dot(xn, w_ref[0, :, base + 2 * LRU_WIDTH:], preferred_element_type=F32)
    ut = lax.dot_general(wt_ref[0], xn, (((1,), (1,)), ((), ())), preferred_element_type=F32)
    for cc in range(tm // ML_CHUNK):
        cols = slice(cc * ML_CHUNK, (cc + 1) * ML_CHUNK)
        for hh in range(ML_HEADS):
            kt_ref[0, hh, cc] = ut[hh * HEAD_DIM:(hh + 1) * HEAD_DIM, cols]
        gt_ref[0, cc] = ut[T_GATE:T_NSAG, cols]
    gn_ref[0] = ut[T_NSAG:T_VS, :]
    for j in range(2 * NSA_GROUPS):
        vt_ref[0, j] = ut[T_VS + j * HEAD_DIM:T_VS + (j + 1) * HEAD_DIM, :]


def _in_proj(h3, gain, w_main, w_t, layer, *, tm=256):
    b, s, d = h3.shape
    n_main = w_main.shape[-1]
    nc = s // ML_CHUNK
    body = functools.partial(_inproj_body, tm=tm)
    return pl.pallas_call(
        body,
        grid=(b, s // tm),
        in_specs=[
            pl.BlockSpec((1, tm, d), lambda bi, ti: (bi, ti, 0)),
            pl.BlockSpec((1, d), lambda bi, ti: (0, 0)),
            pl.BlockSpec((1, d, n_main), lambda bi, ti: (layer, 0, 0)),
            pl.BlockSpec((1, N_T_ROWS, d), lambda bi, ti: (layer, 0, 0)),
        ],
        out_specs=[
            pl.BlockSpec((1, N_SLOTS, tm, HEAD_DIM), lambda bi, ti: (bi, 0, ti, 0)),
            pl.BlockSpec((1, tm, LRU_WIDTH), lambda bi, ti: (bi, ti, 0)),
            pl.BlockSpec((1, tm, LRU_WIDTH), lambda bi, ti: (bi, ti, 0)),
            pl.BlockSpec((1, tm, LANES), lambda bi, ti: (bi, ti, 0)),
            pl.BlockSpec((1, ML_HEADS, tm // ML_CHUNK, HEAD_DIM, ML_CHUNK), lambda bi, ti: (bi, 0, ti, 0, 0)),
            pl.BlockSpec((1, tm // ML_CHUNK, 2 * ML_HEADS, ML_CHUNK), lambda bi, ti: (bi, ti, 0, 0)),
            pl.BlockSpec((1, NSA_HEADS * 3, tm), lambda bi, ti: (bi, 0, ti)),
            pl.BlockSpec((1, 2 * NSA_GROUPS, HEAD_DIM, tm), lambda bi, ti: (bi, 0, 0, ti)),
        ],
        out_shape=[
            jax.ShapeDtypeStruct((b, N_SLOTS, s, HEAD_DIM), F32),
            jax.ShapeDtypeStruct((b, s, LRU_WIDTH), F32),
            jax.ShapeDtypeStruct((b, s, LRU_WIDTH), F32),
            jax.ShapeDtypeStruct((b, s, LANES), F32),
            jax.ShapeDtypeStruct((b, ML_HEADS, nc, HEAD_DIM, ML_CHUNK), F32),
            jax.ShapeDtypeStruct((b, nc, 2 * ML_HEADS, ML_CHUNK), F32),
            jax.ShapeDtypeStruct((b, NSA_HEADS * 3, s), F32),
            jax.ShapeDtypeStruct((b, 2 * NSA_GROUPS, HEAD_DIM, s), F32),
        ],
        compiler_params=_cparams(("parallel", "parallel")),
        name="in_proj",
    )(h3, gain, w_main, w_t)


def _compress_hidden(z_ref, pos_ref, w1_ref):
    n = z_ref.shape[2] // CMP_STRIDE
    top = bot = None
    for c in range(CMP_STRIDE):
        zc = z_ref[0, 0, pl.ds(c, n, stride=CMP_STRIDE), :]
        for half in range(2):
            off = half * CMP_STRIDE + c
            x = (zc + pos_ref[0, off:off + 1, :]).astype(BF16)
            part = jnp.dot(x, w1_ref[0, off * HEAD_DIM:(off + 1) * HEAD_DIM, :], preferred_element_type=F32)
            if half == 0:
                top = part if top is None else top + part
            else:
                bot = part if bot is None else bot + part
    hid = top + pltpu.roll(bot, n - 1, axis=0)
    return (hid * jax.nn.sigmoid(hid)).astype(BF16)


def _compress_body(zk_ref, zv_ref, pos_ref, w1_ref, w2k_ref, w2vt_ref, kc_ref, vct_ref):
    hk = _compress_hidden(zk_ref, pos_ref.at[0:1], w1_ref.at[0:1])
    hv = _compress_hidden(zv_ref, pos_ref.at[1:2], w1_ref.at[1:2])
    kc = jnp.dot(hk, w2k_ref[...], preferred_element_type=F32)
    n = kc.shape[0]
    blk_start = lax.broadcasted_iota(jnp.int32, (n, HEAD_DIM), 0) * CMP_STRIDE
    kc_ref[0, 0] = jnp.concatenate([kc, _position_columns(blk_start, HEAD_DIM)], axis=1).astype(BF16)
    vct = lax.dot_general(w2vt_ref[...], hv, (((1,), (1,)), ((), ())), preferred_element_type=F32)
    vct_ref[0, 0] = vct.astype(BF16)


def _compress(heads, pos, w1, w2k, w2vt):
    b, _, s, _ = heads.shape
    n = s // CMP_STRIDE
    slot = lambda first: pl.BlockSpec((1, 1, s, HEAD_DIM), lambda bi, g: (bi, first + g, 0, 0))
    whole = lambda a: pl.BlockSpec(a.shape, lambda bi, g: (0,) * a.ndim)
    return pl.pallas_call(
        _compress_body,
        grid=(b, NSA_GROUPS),
        in_specs=[slot(SLOT_KC), slot(SLOT_VC), whole(pos), whole(w1), whole(w2k), whole(w2vt)],
        out_specs=[
            pl.BlockSpec((1, 1, n, 2 * HEAD_DIM), lambda bi, g: (bi, g, 0, 0)),
            pl.BlockSpec((1, 1, HEAD_DIM, n), lambda bi, g: (bi, g, 0, 0)),
        ],
        out_shape=[
            jax.ShapeDtypeStruct((b, NSA_GROUPS, n, 2 * HEAD_DIM), BF16),
            jax.ShapeDtypeStruct((b, NSA_GROUPS, HEAD_DIM, n), BF16),
        ],
        compiler_params=_cparams(("parallel", "parallel")),
        name="compress",
    )(heads, heads, pos, w1, w2k, w2vt)


TQ = 128
N_SEL = 32


N_IDX = 3
SEL_KEYS = 512
WIN_KEYS = WINDOW + TQ


def _position_columns(pos, width):
    lane = lax.broadcasted_iota(jnp.int32, (pos.shape[0], width), 1)
    lo = jnp.bitwise_and(pos, LANES - 1).astype(F32)
    hi = jnp.right_shift(pos, int(np.log2(LANES))).astype(F32)
    return jnp.where(lane < N_IDX, lo, jnp.where(lane < 2 * N_IDX, hi, 0.0))


def _nsa_body(q_ref, sl_ref, kc_ref, vct_ref, ks_ref, kw_ref, vst_ref, vwt_ref, gn_ref, ovt_ref, ext_ref,
              o_ref, ksa_sc, kwa_sc, vs_sc, vw_sc):
    g = pl.program_id(1)
    qi = pl.program_id(2)
    s_len = ks_ref.shape[2]
    r_cols = NSA_HPG * TQ
    nt = (((1,), (1,)), ((), ()))

    @pl.when(qi == 0)
    def _():
        for c in range(s_len // TQ):
            rows = slice(c * TQ, (c + 1) * TQ)
            pos = c * TQ + lax.broadcasted_iota(jnp.int32, (TQ, HEAD_DIM), 0)
            idx = _position_columns(pos, HEAD_DIM)
            ksa_sc[rows, :] = jnp.concatenate([ks_ref[0, 0, rows, :], idx], axis=1).astype(BF16)
            kwa_sc[rows, :] = jnp.concatenate([kw_ref[0, 0, rows, :], idx], axis=1).astype(BF16)
        vs_sc[...] = vst_ref[0, 0].astype(BF16)
        vw_sc[...] = vwt_ref[0, 0].astype(BF16)

    q = q_ref[0].reshape(r_cols, HEAD_DIM) * (HEAD_DIM ** -0.5 * LOG2E)
    q_aug = jnp.concatenate([q, sl_ref[0]], axis=1).astype(BF16)

    def tile4(x):
        return jnp.concatenate([x] * NSA_HPG, axis=1)

    def rel_pos(n_keys, start):
        k_loc = lax.broadcasted_iota(jnp.int32, (n_keys, TQ), 0)
        q_loc = lax.broadcasted_iota(jnp.int32, (n_keys, TQ), 1)
        return (q_loc - k_loc) + (qi * TQ - start)

    k_loc = lax.broadcasted_iota(jnp.int32, (TQ, TQ), 0)
    q_loc = lax.broadcasted_iota(jnp.int32, (TQ, TQ), 1)
    s_c = lax.dot_general(kc_ref[0, 0], q_aug, nt, preferred_element_type=F32)
    ok_c = tile4(k_loc * CMP_STRIDE + (CMP_BLOCK - 1) <= qi * TQ + q_loc)
    s_c = jnp.where(ok_c, s_c, NEG)
    m_c = jnp.max(s_c, axis=0, keepdims=True)
    e_c = jnp.where(ok_c, jnp.exp2(s_c - m_c), 0.0)
    l_c = jnp.sum(e_c, axis=0, keepdims=True)
    p_c = e_c / jnp.where(l_c > 0.0, l_c, 1.0)
    o_c = jnp.dot(vct_ref[0, 0], p_c.astype(BF16), preferred_element_type=F32)

    p_sum = p_c[:, 0:TQ]
    for r in range(1, NSA_HPG):
        p_sum = p_sum + p_c[:, r * TQ:(r + 1) * TQ]
    ovt = ovt_ref[...]
    imp = sum(jnp.dot(ovt, part, preferred_element_type=F32) for part in _split3(p_sum))[:N_SEL]
    blk = lax.broadcasted_iota(jnp.int32, (N_SEL, TQ), 0)
    cur = jnp.right_shift(qi * TQ + lax.broadcasted_iota(jnp.int32, (N_SEL, TQ), 1), int(np.log2(SEL_BLOCK)))
    valid = blk <= cur
    forced = valid & ((blk == 0) | (blk > cur - SEL_LOCAL))
    score = jnp.where(valid, imp + jnp.where(forced, FORCE_BONUS, 0.0), NEG)
    rank = jnp.zeros((N_SEL, TQ), F32)
    for jp in range(N_SEL):
        row = score[jp:jp + 1, :]
        beats = (row > score) | ((row == score) & (blk > jp))
        rank = rank + jnp.where(beats, 1.0, 0.0)
    unsel = jnp.where(rank < float(SEL_TOPK), 0.0, 1.0)
    unsel = jnp.concatenate([unsel, jnp.zeros((LANES - N_SEL, TQ), F32)], axis=0).astype(BF16)

    def attend(ka_sc, vt_sc, start, n_keys, carry, extra=None, keep=None):
        m, l, acc = carry
        s = lax.dot_general(ka_sc[pl.ds(start, n_keys), :], q_aug, nt, preferred_element_type=F32)
        if extra is not None:
            s = s + tile4(extra)
        if keep is not None:
            s = jnp.where(tile4(keep), s, NEG)
        m_new = jnp.maximum(m, jnp.max(s, axis=0, keepdims=True))
        p = jnp.exp2(s - m_new)
        alpha = jnp.exp2(m - m_new)
        l = alpha * l + jnp.sum(p, axis=0, keepdims=True)
        pv = jnp.dot(vt_sc[:, pl.ds(start, n_keys)], p.astype(BF16), preferred_element_type=F32)
        return m_new, l, alpha * acc + pv

    init = (jnp.full((1, r_cols), NEG, F32), jnp.zeros((1, r_cols), F32), jnp.zeros((HEAD_DIM, r_cols), F32))

    def sel_step(j, carry, last):
        start = pl.multiple_of(j * SEL_KEYS, SEL_KEYS)
        bias = jnp.dot(ext_ref[pl.ds(start, SEL_KEYS), :], unsel, preferred_element_type=F32)
        keep = rel_pos(SEL_KEYS, start) >= 0 if last else None
        return attend(ksa_sc, vs_sc, start, SEL_KEYS, carry, extra=bias, keep=keep)

    n_full = (qi * TQ) // SEL_KEYS
    carry = lax.fori_loop(0, n_full, lambda j, c: sel_step(j, c, False), init)
    _, l_s, acc_s = sel_step(n_full, carry, True)
    o_s = acc_s / l_s

    w_start = pl.multiple_of(jnp.maximum(qi * TQ - WINDOW, 0), TQ)
    rel = rel_pos(WIN_KEYS, w_start)
    _, l_w, acc_w = attend(kwa_sc, vw_sc, w_start, WIN_KEYS, init, keep=(rel >= 0) & (rel < WINDOW))
    o_w = acc_w / l_w

    gates = jax.nn.sigmoid(gn_ref[0])
    outs = []
    for r in range(NSA_HPG):
        cols = slice(r * TQ, (r + 1) * TQ)
        acc = None
        for br, o_b in enumerate((o_c, o_s, o_w)):
            row_lo = r * 3 + br
            g0 = gates[row_lo:row_lo + 1, :]
            g1 = gates[NSA_HPG * 3 + row_lo:NSA_HPG * 3 + row_lo + 1, :]
            term = jnp.where(g == 0, g0, g1) * o_b[:, cols]
            acc = term if acc is None else acc + term
        outs.append(acc)
    o_ref[0] = jnp.concatenate(outs, axis=0).T


def _nsa(heads, kc_aug, vct, vt, gn, sl, ovt, ext):
    b, _, s, _ = heads.shape
    n_cmp = kc_aug.shape[2]
    assert n_cmp == TQ and s // SEL_BLOCK == N_SEL
    r_cols = NSA_HPG * TQ
    k_spec = lambda slot: pl.BlockSpec((1, 1, s, HEAD_DIM), lambda bi, g, qi: (bi, slot + g, 0, 0))
    vt_spec = lambda first: pl.BlockSpec((1, 1, HEAD_DIM, s), lambda bi, g, qi: (bi, first + g, 0, 0))
    return pl.pallas_call(
        _nsa_body,
        grid=(b, NSA_GROUPS, s // TQ),
        in_specs=[
            pl.BlockSpec((1, NSA_HPG, TQ, HEAD_DIM), lambda bi, g, qi: (bi, g, qi, 0)),
            pl.BlockSpec((1, r_cols, HEAD_DIM), lambda bi, g, qi: (g, 0, 0)),
            pl.BlockSpec((1, 1, n_cmp, 2 * HEAD_DIM), lambda bi, g, qi: (bi, g, 0, 0)),
            pl.BlockSpec((1, 1, HEAD_DIM, n_cmp), lambda bi, g, qi: (bi, g, 0, 0)),
            k_spec(SLOT_KS), k_spec(SLOT_KW), vt_spec(0), vt_spec(NSA_GROUPS),
            pl.BlockSpec((1, NSA_HEADS * 3, TQ), lambda bi, g, qi: (bi, 0, qi)),
            pl.BlockSpec((LANES, LANES), lambda bi, g, qi: (0, 0)),
            pl.BlockSpec((s, LANES), lambda bi, g, qi: (0, 0)),
        ],
        out_specs=pl.BlockSpec((1, TQ, NSA_HPG * HEAD_DIM), lambda bi, g, qi: (bi, qi, g)),
        out_shape=jax.ShapeDtypeStruct((b, s, NSA_WIDTH), F32),
        scratch_shapes=[
            pltpu.VMEM((s, 2 * HEAD_DIM), BF16),
            pltpu.VMEM((s, 2 * HEAD_DIM), BF16),
            pltpu.VMEM((HEAD_DIM, s), BF16),
            pltpu.VMEM((HEAD_DIM, s), BF16),
        ],
        compiler_params=_cparams(("arbitrary", "arbitrary", "arbitrary")),
        name="nsa",
    )(heads, sl, kc_aug, vct, heads, heads, vt, vt, gn, ovt, ext)


SCAN_PAD = 1024
SCAN_ROWS = 256


def _lru_body(x_ref, g_ref, cw_ref, cb_ref, wa_ref, ba_ref, wx_ref, bx_ref, lam_ref, o_ref,
              a0, b0, a1, b1):
    s = x_ref.shape[1]
    x = x_ref[0]
    row = lax.broadcasted_iota(jnp.int32, x.shape, 0)
    xc = cb_ref[...]
    for tap in range(CONV_WIDTH):
        back = CONV_WIDTH - 1 - tap
        xs = x if back == 0 else jnp.where(row >= back, pltpu.roll(x, back, axis=0), 0.0)
        xc = xc + xs * cw_ref[tap:tap + 1, :]
    xb = xc.astype(BF16)
    r = jax.nn.sigmoid(jnp.dot(xb, wa_ref[...], preferred_element_type=F32) + ba_ref[...])
    i = jax.nn.sigmoid(jnp.dot(xb, wx_ref[...], preferred_element_type=F32) + bx_ref[...])
    log_a = -LRU_C * r * jax.nn.softplus(-lam_ref[...])
    a = jnp.exp(log_a)
    u = jnp.sqrt(-jnp.tanh(log_a) * (a * a + 1.0)) * (i * xc)

    bufs = ((a0, b0), (a1, b1))
    for ab, bb in bufs:
        ab[0:SCAN_PAD, :] = jnp.ones((SCAN_PAD, x.shape[1]), F32)
        bb[0:SCAN_PAD, :] = jnp.zeros((SCAN_PAD, x.shape[1]), F32)
    a0[SCAN_PAD:, :] = a
    b0[SCAN_PAD:, :] = u
    n_steps = int(np.log2(s))
    for k in range(n_steps):
        d = 1 << k
        (sa, sb), (da, db) = bufs[k % 2], bufs[(k + 1) % 2]
        for c in range(s // SCAN_ROWS):
            lo = SCAN_PAD + c * SCAN_ROWS
            a_cur = sa[lo:lo + SCAN_ROWS, :]
            a_prev = sa[lo - d:lo - d + SCAN_ROWS, :]
            b_cur = sb[lo:lo + SCAN_ROWS, :]
            b_prev = sb[lo - d:lo - d + SCAN_ROWS, :]
            da[lo:lo + SCAN_ROWS, :] = a_cur * a_prev
            db[lo:lo + SCAN_ROWS, :] = a_cur * b_prev + b_cur
    h = bufs[n_steps % 2][1][SCAN_PAD:, :]
    o_ref[0] = h * jax.nn.gelu(g_ref[0])


def _rglru(xr, gr, conv_w, conv_b, wa, ba, wx, bx, lam):
    b, s, c = xr.shape
    assert s <= 2 * SCAN_PAD and s % SCAN_ROWS == 0
    tok = pl.BlockSpec((1, s, c), lambda bi: (bi, 0, 0))
    vec = lambda n: pl.BlockSpec((n, c), lambda bi: (0, 0))
    return pl.pallas_call(
        _lru_body,
        grid=(b,),
        in_specs=[tok, tok, vec(CONV_WIDTH), vec(1), vec(c), vec(1), vec(c), vec(1), vec(1)],
        out_specs=tok,
        out_shape=jax.ShapeDtypeStruct((b, s, c), F32),
        scratch_shapes=[pltpu.VMEM((SCAN_PAD + s, c), F32)] * 4,
        compiler_params=_cparams(("parallel",)),
        name="rglru",
    )(xr, gr, conv_w, conv_b, wa, ba, wx, bx, lam)


def _mlstm_body(q_ref, v_ref, o_ref_in, kt_ref, gcol_ref, grow_ref, bcol_ref, brow_ref, tri_ref, out_ref):
    nc = q_ref.shape[2]
    L = ML_CHUNK
    ones_col = jnp.where(lax.broadcasted_iota(jnp.int32, (L, HEAD_DIM), 1) == 0, 1.0, 0.0)
    rr = lax.broadcasted_iota(jnp.int32, (L, L), 0)
    cc = lax.broadcasted_iota(jnp.int32, (L, L), 1)
    causal = cc <= rr
    srow = lax.broadcasted_iota(jnp.int32, (L, LANES), 0)

    def chunk(c, carry):
        gcol = gcol_ref[0, c] + bcol_ref[...]
        lf_col = jax.nn.log_sigmoid(gcol)
        a_col = lf_col
        for k in range(int(np.log2(L))):
            d = 1 << k
            a_col = a_col + jnp.where(srow >= d, pltpu.roll(a_col, d, axis=0), 0.0)
        grow = grow_ref[0, c] + brow_ref[...]
        lf_row = jax.nn.log_sigmoid(grow)
        a_rows = _dot_exact01(lf_row, tri_ref[...])
        new_carry = []
        outs = []
        for h in range(ML_HEADS):
            c_aug, m_prev = carry[h]
            q = q_ref[0, h, c].astype(BF16)
            kt = kt_ref[0, h, c]
            v = v_ref[0, h, c]
            v_aug = jnp.concatenate([v, ones_col], axis=1).astype(BF16)
            a_c = a_col[:, GATE_F + h:GATE_F + h + 1]
            a_r = jnp.broadcast_to(a_rows[ML_HEADS + h:ML_HEADS + h + 1, :], (SUBLANES, L))
            ig_r = jnp.broadcast_to(grow[h:h + 1, :], (SUBLANES, L))
            a_end = a_r[:, L - 1:L]
            tall = lambda s8: jnp.concatenate([s8] * (L // SUBLANES), axis=0)

            qk = jnp.dot(q, kt.astype(BF16), preferred_element_type=F32) * HEAD_DIM ** -0.5
            dmat = jnp.where(causal, a_c + (ig_r - a_r)[0:1, :], NEG)
            m_inter = a_c + tall(m_prev)
            m = jnp.maximum(m_inter, jnp.max(dmat, axis=1, keepdims=True))
            inter = jnp.exp(m_inter - m)
            w = jnp.exp(dmat - m) * qk
            num = inter * jnp.dot(q, c_aug.astype(BF16), preferred_element_type=F32)
            num = num + jnp.dot(w.astype(BF16), v_aug, preferred_element_type=F32)
            den = num[:, HEAD_DIM:HEAD_DIM + 1]
            hval = num[:, :HEAD_DIM] / jnp.maximum(jnp.abs(den), jnp.exp(-m))
            outs.append(jax.nn.sigmoid(o_ref_in[0, h, c]) * hval)

            w_end = a_end - a_r + ig_r
            m_new = jnp.maximum(a_end + m_prev, jnp.max(w_end, axis=1, keepdims=True))
            decay = jnp.exp(a_end + m_prev - m_new)
            wk = jnp.exp(w_end - m_new) * HEAD_DIM ** -0.5
            kw = (kt * wk[0:1, :]).astype(BF16)
            c_new = tall(decay) * c_aug + jnp.dot(kw, v_aug, preferred_element_type=F32)
            new_carry.append((c_new, m_new))
        out_ref[0, c] = jnp.concatenate(outs, axis=1)
        return tuple(new_carry)

    init = tuple((jnp.zeros((HEAD_DIM, 2 * HEAD_DIM), F32), jnp.zeros((SUBLANES, 1), F32)) for _ in range(ML_HEADS))
    lax.fori_loop(0, nc, chunk, init, unroll=2)


def _mlstm(heads_c, kt, gates_c, gt, bcol, brow, tri):
    b, _, nc, L, _ = heads_c.shape
    slot_spec = lambda slot: pl.BlockSpec((1, ML_HEADS, nc, L, HEAD_DIM), lambda bi: (bi, slot // ML_HEADS, 0, 0, 0))
    return pl.pallas_call(
        _mlstm_body,
        grid=(b,),
        in_specs=[
            slot_spec(SLOT_QM), slot_spec(SLOT_VM), slot_spec(SLOT_OM),
            pl.BlockSpec((1, ML_HEADS, nc, HEAD_DIM, L), lambda bi: (bi, 0, 0, 0, 0)),
            pl.BlockSpec((1, nc, L, LANES), lambda bi: (bi, 0, 0, 0)),
            pl.BlockSpec((1, nc, 2 * ML_HEADS, L), lambda bi: (bi, 0, 0, 0)),
            pl.BlockSpec((1, LANES), lambda bi: (0, 0)),
            pl.BlockSpec((2 * ML_HEADS, 1), lambda bi: (0, 0)),
            pl.BlockSpec((L, L), lambda bi: (0, 0)),
        ],
        out_specs=pl.BlockSpec((1, nc, L, ML_WIDTH), lambda bi: (bi, 0, 0, 0)),
        out_shape=jax.ShapeDtypeStruct((b, nc, L, ML_WIDTH), F32),
        compiler_params=_cparams(("parallel",)),
        name="mlstm",
    )(heads_c, heads_c, heads_c, kt, gates_c, gt, bcol, brow, tri)


def _outproj_body(h_ref, on_ref, ol_ref, om_ref, gn_ref, ind_ref, indt_ref, w_ref, o_ref):
    acc = h_ref[...]
    col = 0
    for x_ref in (on_ref, ol_ref, om_ref):
        x = x_ref[...]
        width = x.shape[1]
        ind = ind_ref[col:col + width, :]
        ms = _dot_exact01(x * x, ind) * (1.0 / HEAD_DIM)
        scale = _dot_exact01(lax.rsqrt(ms + EPS), indt_ref[:, col:col + width])
        xn = (x * scale * gn_ref[:, col:col + width]).astype(BF16)
        acc = acc + jnp.dot(xn, w_ref[0, col:col + width, :], preferred_element_type=F32)
        col += width
    o_ref[...] = acc


def _out_proj(h, o_nsa, o_lru, o_ml, gain, ind, indt, w_out, layer, *, tm=512):
    m, d = h.shape
    tok = lambda w: pl.BlockSpec((tm, w), lambda i: (i, 0))
    return pl.pallas_call(
        _outproj_body,
        grid=(m // tm,),
        in_specs=[
            tok(d), tok(o_nsa.shape[1]), tok(o_lru.shape[1]), tok(o_ml.shape[1]),
            pl.BlockSpec((1, d), lambda i: (0, 0)),
            pl.BlockSpec((d, LANES), lambda i: (0, 0)),
            pl.BlockSpec((LANES, d), lambda i: (0, 0)),
            pl.BlockSpec((1, d, d), lambda i: (layer, 0, 0)),
        ],
        out_specs=tok(d),
        out_shape=jax.ShapeDtypeStruct((m, d), F32),
        compiler_params=_cparams(("parallel",)),
        name="out_proj",
    )(h, o_nsa, o_lru, o_ml, gain, ind, indt, w_out)


def _pack_w_in(w_in):
    cols = np.concatenate([
        np.arange(_OFF_Q, _OFF_G),
        np.arange(_OFF_QM, _OFF_IM),
        np.arange(_OFF_OM, D_IN),
        np.arange(_OFF_XR, _OFF_QM),
        np.arange(_OFF_G, _OFF_XR),
        np.arange(_OFF_IM, _OFF_OM),
    ])
    main = w_in[:, :, cols]
    pad = LANES - (NSA_HEADS * 3 + 2 * ML_HEADS)
    main = jnp.pad(main, ((0, 0), (0, 0), (0, pad))).astype(BF16)
    off_vs = _OFF_KV + 3 * NSA_KV_WIDTH
    off_vw = _OFF_KV + 5 * NSA_KV_WIDTH
    t_cols = np.concatenate([np.arange(_OFF_KM, _OFF_VM), np.arange(_OFF_IM, _OFF_OM), np.arange(_OFF_G, _OFF_XR),
                             np.arange(off_vs, off_vs + NSA_KV_WIDTH), np.arange(off_vw, off_vw + NSA_KV_WIDTH)])
    w_t = jnp.swapaxes(w_in[:, :, t_cols], 1, 2).astype(BF16)
    return main, w_t


def _block_diag(w):
    depth, n, c, _ = w.shape
    eye = jnp.eye(n, dtype=w.dtype)
    return jnp.einsum("lncd,nm->lncmd", w, eye).reshape(depth, n * c, n * c)


def _constants(s):
    n_cmp_pad = s // CMP_STRIDE
    n = np.arange(n_cmp_pad)
    j = np.arange(LANES)
    cmp_start = n * CMP_STRIDE
    cmp_end = cmp_start + CMP_BLOCK - 1
    slc_start = j * SEL_BLOCK
    n_cmp = (s - CMP_BLOCK) // CMP_STRIDE + 1
    ov = ((cmp_start[:, None] <= slc_start[None, :] + SEL_BLOCK - 1) & (cmp_end[:, None] >= slc_start[None, :])
          & (n[:, None] < n_cmp) & (j[None, :] < s // SEL_BLOCK))
    key_block = np.arange(s) // SEL_BLOCK
    ext = np.where(key_block[:, None] == j[None, :], NEG, 0.0)
    tri = np.triu(np.ones((ML_CHUNK, ML_CHUNK)))
    heads = np.arange(D_MODEL) // HEAD_DIM
    ind = (heads[:, None] == j[None, :])
    slope = 2.0 ** (-8.0 * np.arange(1, NSA_HEADS + 1) / NSA_HEADS) * LOG2E
    parts = jnp.stack(_split3(jnp.asarray(slope, F32)), axis=1).astype(F32)
    cols = jnp.concatenate([parts, parts * float(LANES)], axis=1)
    cols = jnp.pad(cols, ((0, 0), (0, HEAD_DIM - 2 * N_IDX)))
    sl = jnp.repeat(cols, TQ, axis=0).reshape(NSA_GROUPS, NSA_HPG * TQ, HEAD_DIM)
    return dict(ovt=jnp.asarray(ov.T, BF16), ext=jnp.asarray(ext, BF16), tri=jnp.asarray(tri, BF16),
                ind=jnp.asarray(ind, BF16), indt=jnp.asarray(ind.T, BF16), sl=sl)


def _mixers(h3, layer, p):
    b, s, d = h3.shape
    heads, xr, gr, gates, kt, gt, gn, vt = _in_proj(h3, p["mix_norm"][layer][None], p["w_main"], p["w_t"], layer)

    kc_aug, vct = _compress(heads, p["cmp_pos"][layer], p["cmp_w1"][layer], p["cmp_w2k"][layer], p["cmp_w2vt"][layer])
    o_nsa = _nsa(heads, kc_aug, vct, vt, gn, p["sl"], p["ovt"], p["ext"])

    o_lru = _rglru(xr, gr, p["conv_w"][layer], p["conv_b"][layer][None], p["wa"][layer], p["ba"][layer][None],
                   p["wx"][layer], p["bx"][layer][None], p["lam"][layer][None])

    nc = s // ML_CHUNK
    heads_c = heads.reshape(b, N_SLOTS, nc, ML_CHUNK, HEAD_DIM)
    gates_c = gates.reshape(b, nc, ML_CHUNK, LANES)
    o_ml = _mlstm(heads_c, kt, gates_c, gt, p["ml_bcol"][layer][None], p["ml_brow"][layer][:, None], p["tri"])
    m = b * s
    return o_nsa.reshape(m, NSA_WIDTH), o_lru.reshape(m, LRU_WIDTH), o_ml.reshape(m, ML_WIDTH)


def kernel(x, ffn1_norm, ffn1_w1, ffn1_w3, ffn1_w2, mix_norm, w_in, nsa_cmp_pos_k, nsa_cmp_w1_k, nsa_cmp_w2_k, nsa_cmp_pos_v, nsa_cmp_w1_v, nsa_cmp_w2_v, lru_conv_w, lru_conv_b, lru_w_a, lru_b_a, lru_w_x, lru_b_x, lru_lambda, ml_b_i, ml_b_f, head_norm, w_out, ffn2_norm, ffn2_w1, ffn2_w3, ffn2_w2, final_norm):
    b, s, d = x.shape
    m = b * s
    depth = w_in.shape[0]
    consts = _constants(s)
    ind, indt = consts["ind"], consts["indt"]
    w_main, w_t = _pack_w_in(w_in)
    zeros_gate = jnp.zeros((depth, GATE_I), F32)
    bias_lanes = jnp.concatenate([zeros_gate, ml_b_i, ml_b_f], axis=1)
    p = {
        **consts,
        "mix_norm": mix_norm, "w_main": w_main, "w_t": w_t,
        "cmp_pos": jnp.stack([nsa_cmp_pos_k, nsa_cmp_pos_v], axis=1),
        "cmp_w1": jnp.stack([nsa_cmp_w1_k, nsa_cmp_w1_v], axis=1).astype(BF16),
        "cmp_w2k": nsa_cmp_w2_k.astype(BF16),
        "cmp_w2vt": jnp.swapaxes(nsa_cmp_w2_v, 1, 2).astype(BF16),
        "conv_w": lru_conv_w, "conv_b": lru_conv_b,
        "wa": _block_diag(lru_w_a).astype(BF16), "ba": lru_b_a.reshape(depth, LRU_WIDTH),
        "wx": _block_diag(lru_w_x).astype(BF16), "bx": lru_b_x.reshape(depth, LRU_WIDTH),
        "lam": lru_lambda,
        "ml_bcol": jnp.pad(bias_lanes, ((0, 0), (0, LANES - bias_lanes.shape[1]))),
        "ml_brow": jnp.concatenate([ml_b_i, ml_b_f], axis=1),
    }
    f1 = [w.astype(BF16) for w in (ffn1_w1, ffn1_w3, ffn1_w2)]
    f2 = [w.astype(BF16) for w in (ffn2_w1, ffn2_w3, ffn2_w2)]
    w_out_b = w_out.astype(BF16)
    gain_heads = head_norm.reshape(depth, 1, d)

    h = x.reshape(m, d)
    for layer in range(depth):
        h = _ffn(h, ffn1_norm[layer][None], *f1, layer)
        o_nsa, o_lru, o_ml = _mixers(h.reshape(b, s, d), layer, p)
        h = _out_proj(h, o_nsa, o_lru, o_ml, gain_heads[layer], ind, indt, w_out_b, layer)
        last = layer == depth - 1
        h = _ffn(h, ffn2_norm[layer][None], *f2, layer, final_norm[None] if last else None)
    return h.reshape(b, s, d)
```

```python
import functools

import jax
import jax.numpy as jnp
import numpy as np
from jax import lax
from jax.experimental import pallas as pl
from jax.experimental.pallas import tpu as pltpu

F32 = jnp.float32
BF16 = jnp.bfloat16

D_MODEL = 1024
DEPTH = 4
HEAD_DIM = 64
NSA_HEADS = 8
NSA_GROUPS = 2
NSA_HPG = NSA_HEADS // NSA_GROUPS
NSA_WIDTH = NSA_HEADS * HEAD_DIM
NSA_KV_WIDTH = NSA_GROUPS * HEAD_DIM
CMP_BLOCK = 32
CMP_STRIDE = 16
CMP_HIDDEN = 256
SEL_BLOCK = 64
SEL_TOPK = 16
SEL_LOCAL = 2
WINDOW = 512
FORCE_BONUS = 1.0e4
LRU_BLOCKS = 4
LRU_WIDTH = LRU_BLOCKS * HEAD_DIM
CONV_WIDTH = 4
LRU_C = 8.0
ML_HEADS = 4
ML_WIDTH = ML_HEADS * HEAD_DIM
ML_CHUNK = 64
D_FF = 2816
EPS = 1e-6
NEG = -1e30

LANES = 128
SUBLANES = 8
VMEM_LIMIT = 52 * 1024 * 1024

_OFF_Q = 0
_OFF_KV = NSA_WIDTH
_OFF_G = _OFF_KV + 6 * NSA_KV_WIDTH
_OFF_XR = _OFF_G + NSA_HEADS * 3
_OFF_GR = _OFF_XR + LRU_WIDTH
_OFF_QM = _OFF_GR + LRU_WIDTH
_OFF_KM = _OFF_QM + ML_WIDTH
_OFF_VM = _OFF_KM + ML_WIDTH
_OFF_IM = _OFF_VM + ML_WIDTH
_OFF_FM = _OFF_IM + ML_HEADS
_OFF_OM = _OFF_FM + ML_HEADS
D_IN = _OFF_OM + ML_WIDTH

N_SLOTS = 36
SLOT_Q, SLOT_KC, SLOT_VC, SLOT_KS, SLOT_VS, SLOT_KW, SLOT_VW = 0, 8, 10, 12, 14, 16, 18
SLOT_QM, SLOT_KM, SLOT_VM, SLOT_OM = 20, 24, 28, 32
GATE_I, GATE_F = NSA_HEADS * 3, NSA_HEADS * 3 + ML_HEADS
T_GATE = ML_WIDTH
T_NSAG = T_GATE + 2 * ML_HEADS
T_VS = T_NSAG + NSA_HEADS * 3
T_VW = T_VS + NSA_KV_WIDTH
N_T_ROWS = T_VW + NSA_KV_WIDTH
LOG2E = 1.4426950408889634


def _cparams(sem):
    return pltpu.CompilerParams(dimension_semantics=sem, vmem_limit_bytes=VMEM_LIMIT)


def _rms(x, g):
    return x * lax.rsqrt(jnp.mean(x * x, axis=-1, keepdims=True) + EPS) * g


def _split3(x):
    hi = x.astype(BF16)
    r1 = x - hi.astype(F32)
    mid = r1.astype(BF16)
    lo = (r1 - mid.astype(F32)).astype(BF16)
    return hi, mid, lo


def _dot_exact01(x, m01):
    hi, mid, lo = _split3(x)
    d = lambda a: jnp.dot(a, m01, preferred_element_type=F32)
    return d(hi) + d(mid) + d(lo)


def _ffn_body(h_ref, g_ref, w1_ref, w3_ref, w2_ref, fg_ref, o_ref, xn_ref, acc_ref, *, n_f, final):
    j = pl.program_id(1)

    @pl.when(j == 0)
    def _():
        xn_ref[...] = _rms(h_ref[...], g_ref[...]).astype(BF16)

    xn = xn_ref[...]
    a = jnp.dot(xn, w1_ref[0], preferred_element_type=F32)
    b = jnp.dot(xn, w3_ref[0], preferred_element_type=F32)
    gated = (a * jax.nn.sigmoid(a) * b).astype(BF16)
    part = jnp.dot(gated, w2_ref[0], preferred_element_type=F32)

    @pl.when(j == 0)
    def _():
        acc_ref[...] = part

    @pl.when(j > 0)
    def _():
        acc_ref[...] += part

    @pl.when(j == n_f - 1)
    def _():
        y = h_ref[...] + 0.5 * acc_ref[...]
        if final:
            y = _rms(y, fg_ref[...])
        o_ref[...] = y


def _ffn(h, gain, w1, w3, w2, layer, final_gain=None, *, tm=512, tf=1408):
    m, d = h.shape
    n_f = D_FF // tf
    final = final_gain is not None
    fg = final_gain if final else gain
    body = functools.partial(_ffn_body, n_f=n_f, final=final)
    return pl.pallas_call(
        body,
        grid=(m // tm, n_f),
        in_specs=[
            pl.BlockSpec((tm, d), lambda i, j: (i, 0)),
            pl.BlockSpec((1, d), lambda i, j: (0, 0)),
            pl.BlockSpec((1, d, tf), lambda i, j: (layer, 0, j)),
            pl.BlockSpec((1, d, tf), lambda i, j: (layer, 0, j)),
            pl.BlockSpec((1, tf, d), lambda i, j: (layer, j, 0)),
            pl.BlockSpec((1, d), lambda i, j: (0, 0)),
        ],
        out_specs=pl.BlockSpec((tm, d), lambda i, j: (i, 0)),
        out_shape=jax.ShapeDtypeStruct((m, d), F32),
        scratch_shapes=[pltpu.VMEM((tm, d), BF16), pltpu.VMEM((tm, d), F32)],
        compiler_params=_cparams(("parallel", "arbitrary")),
        name="ffn",
    )(h, gain, w1, w3, w2, fg)


def _inproj_body(h_ref, g_ref, w_ref, wt_ref, heads_ref, xr_ref, gr_ref, gates_ref, kt_ref, gt_ref, gn_ref, vt_ref,
                 *, tm):
    xn = _rms(h_ref[0], g_ref[...]).astype(BF16)
    per = 2 * LANES // HEAD_DIM
    for c in range(N_SLOTS // per):
        u = jnp.dot(xn, w_ref[0, :, c * 2 * LANES:(c + 1) * 2 * LANES], preferred_element_type=F32)
        for i in range(per):
            heads_ref[0, c * per + i] = u[:, i * HEAD_DIM:(i + 1) * HEAD_DIM]
    base = N_SLOTS * HEAD_DIM
    xr_ref[0] = jnp.dot(xn, w_ref[0, :, base:base + LRU_WIDTH], preferred_element_type=F32)
    gr_ref[0] = jnp.dot(xn, w_ref[0, :, base + LRU_WIDTH:base + 2 * LRU_WIDTH], preferred_element_type=F32)
    gates_ref[0] = jnp.dot(xn, w_ref[0, :, base + 2 * LRU_WIDTH:], preferred_element_type=F32)
    ut = lax.dot_general(wt_ref[0], xn, (((1,), (1,)), ((), ())), preferred_element_type=F32)
    for cc in range(tm // ML_CHUNK):
        cols = slice(cc * ML_CHUNK, (cc + 1) * ML_CHUNK)
        for hh in range(ML_HEADS):
            kt_ref[0, hh, cc] = ut[hh * HEAD_DIM:(hh + 1) * HEAD_DIM, cols]
        gt_ref[0, cc] = ut[T_GATE:T_NSAG, cols]
    gn_ref[0] = ut[T_NSAG:T_VS, :]
    for j in range(2 * NSA_GROUPS):
        vt_ref[0, j] = ut[T_VS + j * HEAD_DIM:T_VS + (j + 1) * HEAD_DIM, :]


def _in_proj(h3, gain, w_main, w_t, layer, *, tm=512):
    b, s, d = h3.shape
    n_main = w_main.shape[-1]
    nc = s // ML_CHUNK
    body = functools.partial(_inproj_body, tm=tm)
    return pl.pallas_call(
        body,
        grid=(b, s // tm),
        in_specs=[
            pl.BlockSpec((1, tm, d), lambda bi, ti: (bi, ti, 0)),
            pl.BlockSpec((1, d), lambda bi, ti: (0, 0)),
            pl.BlockSpec((1, d, n_main), lambda bi, ti: (layer, 0, 0)),
            pl.BlockSpec((1, N_T_ROWS, d), lambda bi, ti: (layer, 0, 0)),
        ],
        out_specs=[
            pl.BlockSpec((1, N_SLOTS, tm, HEAD_DIM), lambda bi, ti: (bi, 0, ti, 0)),
            pl.BlockSpec((1, tm, LRU_WIDTH), lambda bi, ti: (bi, ti, 0)),
            pl.BlockSpec((1, tm, LRU_WIDTH), lambda bi, ti: (bi, ti, 0)),
            pl.BlockSpec((1, tm, LANES), lambda bi, ti: (bi, ti, 0)),
            pl.BlockSpec((1, ML_HEADS, tm // ML_CHUNK, HEAD_DIM, ML_CHUNK), lambda bi, ti: (bi, 0, ti, 0, 0)),
            pl.BlockSpec((1, tm // ML_CHUNK, 2 * ML_HEADS, ML_CHUNK), lambda bi, ti: (bi, ti, 0, 0)),
            pl.BlockSpec((1, NSA_HEADS * 3, tm), lambda bi, ti: (bi, 0, ti)),
            pl.BlockSpec((1, 2 * NSA_GROUPS, HEAD_DIM, tm), lambda bi, ti: (bi, 0, 0, ti)),
        ],
        out_shape=[
            jax.ShapeDtypeStruct((b, N_SLOTS, s, HEAD_DIM), F32),
            jax.ShapeDtypeStruct((b, s, LRU_WIDTH), F32),
            jax.ShapeDtypeStruct((b, s, LRU_WIDTH), F32),
            jax.ShapeDtypeStruct((b, s, LANES), F32),
            jax.ShapeDtypeStruct((b, ML_HEADS, nc, HEAD_DIM, ML_CHUNK), F32),
            jax.ShapeDtypeStruct((b, nc, 2 * ML_HEADS, ML_CHUNK), F32),
            jax.ShapeDtypeStruct((b, NSA_HEADS * 3, s), F32),
            jax.ShapeDtypeStruct((b, 2 * NSA_GROUPS, HEAD_DIM, s), F32),
        ],
        compiler_params=_cparams(("parallel", "parallel")),
        name="in_proj",
    )(h3, gain, w_main, w_t)


def _compress_hidden(z_ref, pos_ref, w1_ref):
    n = z_ref.shape[2] // CMP_STRIDE
    top = bot = None
    for c in range(CMP_STRIDE):
        zc = z_ref[0, 0, pl.ds(c, n, stride=CMP_STRIDE), :]
        for half in range(2):
            off = half * CMP_STRIDE + c
            x = (zc + pos_ref[0, off:off + 1, :]).astype(BF16)
            part = jnp.dot(x, w1_ref[0, off * HEAD_DIM:(off + 1) * HEAD_DIM, :], preferred_element_type=F32)
            if half == 0:
                top = part if top is None else top + part
            else:
                bot = part if bot is None else bot + part
    hid = top + pltpu.roll(bot, n - 1, axis=0)
    return (hid * jax.nn.sigmoid(hid)).astype(BF16)


def _compress_body(zk_ref, zv_ref, pos_ref, w1_ref, w2k_ref, w2vt_ref, kc_ref, vct_ref):
    hk = _compress_hidden(zk_ref, pos_ref.at[0:1], w1_ref.at[0:1])
    hv = _compress_hidden(zv_ref, pos_ref.at[1:2], w1_ref.at[1:2])
    kc = jnp.dot(hk, w2k_ref[...], preferred_element_type=F32)
    n = kc.shape[0]
    blk_start = lax.broadcasted_iota(jnp.int32, (n, HEAD_DIM), 0) * CMP_STRIDE
    kc_ref[0, 0] = jnp.concatenate([kc, _position_columns(blk_start, HEAD_DIM)], axis=1).astype(BF16)
    vct = lax.dot_general(w2vt_ref[...], hv, (((1,), (1,)), ((), ())), preferred_element_type=F32)
    vct_ref[0, 0] = vct.astype(BF16)


def _compress(heads, pos, w1, w2k, w2vt):
    b, _, s, _ = heads.shape
    n = s // CMP_STRIDE
    slot = lambda first: pl.BlockSpec((1, 1, s, HEAD_DIM), lambda bi, g: (bi, first + g, 0, 0))
    whole = lambda a: pl.BlockSpec(a.shape, lambda bi, g: (0,) * a.ndim)
    return pl.pallas_call(
        _compress_body,
        grid=(b, NSA_GROUPS),
        in_specs=[slot(SLOT_KC), slot(SLOT_VC), whole(pos), whole(w1), whole(w2k), whole(w2vt)],
        out_specs=[
            pl.BlockSpec((1, 1, n, 2 * HEAD_DIM), lambda bi, g: (bi, g, 0, 0)),
            pl.BlockSpec((1, 1, HEAD_DIM, n), lambda bi, g: (bi, g, 0, 0)),
        ],
        out_shape=[
            jax.ShapeDtypeStruct((b, NSA_GROUPS, n, 2 * HEAD_DIM), BF16),
            jax.ShapeDtypeStruct((b, NSA_GROUPS, HEAD_DIM, n), BF16),
        ],
        compiler_params=_cparams(("parallel", "parallel")),
        name="compress",
    )(heads, heads, pos, w1, w2k, w2vt)


TQ = 256
N_SEL = 32


N_IDX = 3
SEL_KEYS = 512
WIN_KEYS = WINDOW + TQ


def _position_columns(pos, width):
    lane = lax.broadcasted_iota(jnp.int32, (pos.shape[0], width), 1)
    lo = jnp.bitwise_and(pos, LANES - 1).astype(F32)
    hi = jnp.right_shift(pos, int(np.log2(LANES))).astype(F32)
    return jnp.where(lane < N_IDX, lo, jnp.where(lane < 2 * N_IDX, hi, 0.0))


def _nsa_body(q_ref, sl_ref, kc_ref, vct_ref, ks_ref, kw_ref, vst_ref, vwt_ref, gn_ref, ovt_ref, ext_ref,
              o_ref, ksa_sc, kwa_sc, vs_sc, vw_sc):
    g = pl.program_id(1)
    qi = pl.program_id(2)
    s_len = ks_ref.shape[2]
    r_cols = NSA_HPG * TQ
    nt = (((1,), (1,)), ((), ()))

    @pl.when(qi == 0)
    def _():
        for c in range(s_len // TQ):
            rows = slice(c * TQ, (c + 1) * TQ)
            pos = c * TQ + lax.broadcasted_iota(jnp.int32, (TQ, HEAD_DIM), 0)
            idx = _position_columns(pos, HEAD_DIM)
            ksa_sc[rows, :] = jnp.concatenate([ks_ref[0, 0, rows, :], idx], axis=1).astype(BF16)
            kwa_sc[rows, :] = jnp.concatenate([kw_ref[0, 0, rows, :], idx], axis=1).astype(BF16)
        vs_sc[...] = vst_ref[0, 0].astype(BF16)
        vw_sc[...] = vwt_ref[0, 0].astype(BF16)

    q = q_ref[0].reshape(r_cols, HEAD_DIM) * (HEAD_DIM ** -0.5 * LOG2E)
    q_aug = jnp.concatenate([q, sl_ref[0]], axis=1).astype(BF16)

    def tile4(x):
        return jnp.concatenate([x] * NSA_HPG, axis=1)

    def rel_pos(n_keys, start):
        k_loc = lax.broadcasted_iota(jnp.int32, (n_keys, TQ), 0)
        q_loc = lax.broadcasted_iota(jnp.int32, (n_keys, TQ), 1)
        return (q_loc - k_loc) + (qi * TQ - start)

    n_cmp = kc_ref.shape[2]
    k_loc = lax.broadcasted_iota(jnp.int32, (n_cmp, TQ), 0)
    q_loc = lax.broadcasted_iota(jnp.int32, (n_cmp, TQ), 1)
    s_c = lax.dot_general(kc_ref[0, 0], q_aug, nt, preferred_element_type=F32)
    ok_c = tile4(k_loc * CMP_STRIDE + (CMP_BLOCK - 1) <= qi * TQ + q_loc)
    s_c = jnp.where(ok_c, s_c, NEG)
    m_c = jnp.max(s_c, axis=0, keepdims=True)
    e_c = jnp.where(ok_c, jnp.exp2(s_c - m_c), 0.0)
    l_c = jnp.sum(e_c, axis=0, keepdims=True)
    p_c = e_c / jnp.where(l_c > 0.0, l_c, 1.0)
    o_c = jnp.dot(vct_ref[0, 0], p_c.astype(BF16), preferred_element_type=F32)

    p_sum = p_c[:, 0:TQ]
    for r in range(1, NSA_HPG):
        p_sum = p_sum + p_c[:, r * TQ:(r + 1) * TQ]
    ovt = ovt_ref[...]
    imp = sum(jnp.dot(ovt, part, preferred_element_type=F32) for part in _split3(p_sum))[:N_SEL]
    blk = lax.broadcasted_iota(jnp.int32, (N_SEL, TQ), 0)
    cur = jnp.right_shift(qi * TQ + lax.broadcasted_iota(jnp.int32, (N_SEL, TQ), 1), int(np.log2(SEL_BLOCK)))
    valid = blk <= cur
    forced = valid & ((blk == 0) | (blk > cur - SEL_LOCAL))
    score = jnp.where(valid, imp + jnp.where(forced, FORCE_BONUS, 0.0), NEG)
    rank = jnp.zeros((N_SEL, TQ), F32)
    for jp in range(N_SEL):
        row = score[jp:jp + 1, :]
        beats = (row > score) | ((row == score) & (blk > jp))
        rank = rank + jnp.where(beats, 1.0, 0.0)
    unsel = jnp.where(rank < float(SEL_TOPK), 0.0, 1.0)
    unsel = jnp.concatenate([unsel, jnp.zeros((LANES - N_SEL, TQ), F32)], axis=0).astype(BF16)

    def attend(ka_sc, vt_sc, start, n_keys, carry, extra=None, keep=None):
        m, l, acc = carry
        s = lax.dot_general(ka_sc[pl.ds(start, n_keys), :], q_aug, nt, preferred_element_type=F32)
        if extra is not None:
            s = s + tile4(extra)
        if keep is not None:
            s = jnp.where(tile4(keep), s, NEG)
        m_new = jnp.maximum(m, jnp.max(s, axis=0, keepdims=True))
        p = jnp.exp2(s - m_new)
        alpha = jnp.exp2(m - m_new)
        l = alpha * l + jnp.sum(p, axis=0, keepdims=True)
        pv = jnp.dot(vt_sc[:, pl.ds(start, n_keys)], p.astype(BF16), preferred_element_type=F32)
        return m_new, l, alpha * acc + pv

    init = (jnp.full((1, r_cols), NEG, F32), jnp.zeros((1, r_cols), F32), jnp.zeros((HEAD_DIM, r_cols), F32))

    def sel_step(j, carry, last):
        start = pl.multiple_of(j * SEL_KEYS, SEL_KEYS)
        bias = jnp.dot(ext_ref[pl.ds(start, SEL_KEYS), :], unsel, preferred_element_type=F32)
        keep = rel_pos(SEL_KEYS, start) >= 0 if last else None
        return attend(ksa_sc, vs_sc, start, SEL_KEYS, carry, extra=bias, keep=keep)

    n_full = (qi * TQ) // SEL_KEYS
    carry = lax.fori_loop(0, n_full, lambda j, c: sel_step(j, c, False), init)
    _, l_s, acc_s = sel_step(n_full, carry, True)
    o_s = acc_s / l_s

    w_start = pl.multiple_of(jnp.maximum(qi * TQ - WINDOW, 0), TQ)
    rel = rel_pos(WIN_KEYS, w_start)
    _, l_w, acc_w = attend(kwa_sc, vw_sc, w_start, WIN_KEYS, init, keep=(rel >= 0) & (rel < WINDOW))
    o_w = acc_w / l_w

    gates = jax.nn.sigmoid(gn_ref[0])
    outs = []
    for r in range(NSA_HPG):
        cols = slice(r * TQ, (r + 1) * TQ)
        acc = None
        for br, o_b in enumerate((o_c, o_s, o_w)):
            row_lo = r * 3 + br
            g0 = gates[row_lo:row_lo + 1, :]
            g1 = gates[NSA_HPG * 3 + row_lo:NSA_HPG * 3 + row_lo + 1, :]
            term = jnp.where(g == 0, g0, g1) * o_b[:, cols]
            acc = term if acc is None else acc + term
        outs.append(acc)
    o_ref[0] = jnp.concatenate(outs, axis=0).T


def _nsa(heads, kc_aug, vct, vt, gn, sl, ovt, ext):
    b, _, s, _ = heads.shape
    n_cmp = kc_aug.shape[2]
    assert n_cmp == LANES and s // SEL_BLOCK == N_SEL and s % SEL_KEYS == 0
    r_cols = NSA_HPG * TQ
    k_spec = lambda slot: pl.BlockSpec((1, 1, s, HEAD_DIM), lambda bi, g, qi: (bi, slot + g, 0, 0))
    vt_spec = lambda first: pl.BlockSpec((1, 1, HEAD_DIM, s), lambda bi, g, qi: (bi, first + g, 0, 0))
    return pl.pallas_call(
        _nsa_body,
        grid=(b, NSA_GROUPS, s // TQ),
        in_specs=[
            pl.BlockSpec((1, NSA_HPG, TQ, HEAD_DIM), lambda bi, g, qi: (bi, g, qi, 0)),
            pl.BlockSpec((1, r_cols, HEAD_DIM), lambda bi, g, qi: (g, 0, 0)),
            pl.BlockSpec((1, 1, n_cmp, 2 * HEAD_DIM), lambda bi, g, qi: (bi, g, 0, 0)),
            pl.BlockSpec((1, 1, HEAD_DIM, n_cmp), lambda bi, g, qi: (bi, g, 0, 0)),
            k_spec(SLOT_KS), k_spec(SLOT_KW), vt_spec(0), vt_spec(NSA_GROUPS),
            pl.BlockSpec((1, NSA_HEADS * 3, TQ), lambda bi, g, qi: (bi, 0, qi)),
            pl.BlockSpec((LANES, LANES), lambda bi, g, qi: (0, 0)),
            pl.BlockSpec((s, LANES), lambda bi, g, qi: (0, 0)),
        ],
        out_specs=pl.BlockSpec((1, TQ, NSA_HPG * HEAD_DIM), lambda bi, g, qi: (bi, qi, g)),
        out_shape=jax.ShapeDtypeStruct((b, s, NSA_WIDTH), F32),
        scratch_shapes=[
            pltpu.VMEM((s, 2 * HEAD_DIM), BF16),
            pltpu.VMEM((s, 2 * HEAD_DIM), BF16),
            pltpu.VMEM((HEAD_DIM, s), BF16),
            pltpu.VMEM((HEAD_DIM, s), BF16),
        ],
        compiler_params=_cparams(("arbitrary", "arbitrary", "arbitrary")),
        name="nsa",
    )(heads, sl, kc_aug, vct, heads, heads, vt, vt, gn, ovt, ext)


SCAN_PAD = 1024
SCAN_ROWS = 256


def _lru_body(x_ref, g_ref, cw_ref, cb_ref, wa_ref, ba_ref, wx_ref, bx_ref, lam_ref, o_ref,
              a0, b0, a1, b1):
    s = x_ref.shape[1]
    x = x_ref[0]
    row = lax.broadcasted_iota(jnp.int32, x.shape, 0)
    xc = cb_ref[...]
    for tap in range(CONV_WIDTH):
        back = CONV_WIDTH - 1 - tap
        xs = x if back == 0 else jnp.where(row >= back, pltpu.roll(x, back, axis=0), 0.0)
        xc = xc + xs * cw_ref[tap:tap + 1, :]
    xb = xc.astype(BF16)
    r = jax.nn.sigmoid(jnp.dot(xb, wa_ref[...], preferred_element_type=F32) + ba_ref[...])
    i = jax.nn.sigmoid(jnp.dot(xb, wx_ref[...], preferred_element_type=F32) + bx_ref[...])
    log_a = -LRU_C * r * jax.nn.softplus(-lam_ref[...])
    a = jnp.exp(log_a)
    u = jnp.sqrt(-jnp.tanh(log_a) * (a * a + 1.0)) * (i * xc)

    bufs = ((a0, b0), (a1, b1))
    for ab, bb in bufs:
        ab[0:SCAN_PAD, :] = jnp.ones((SCAN_PAD, x.shape[1]), F32)
        bb[0:SCAN_PAD, :] = jnp.zeros((SCAN_PAD, x.shape[1]), F32)
    a0[SCAN_PAD:, :] = a
    b0[SCAN_PAD:, :] = u
    n_steps = int(np.log2(s))
    for k in range(n_steps):
        d = 1 << k
        (sa, sb), (da, db) = bufs[k % 2], bufs[(k + 1) % 2]
        for c in range(s // SCAN_ROWS):
            lo = SCAN_PAD + c * SCAN_ROWS
            a_cur = sa[lo:lo + SCAN_ROWS, :]
            a_prev = sa[lo - d:lo - d + SCAN_ROWS, :]
            b_cur = sb[lo:lo + SCAN_ROWS, :]
            b_prev = sb[lo - d:lo - d + SCAN_ROWS, :]
            da[lo:lo + SCAN_ROWS, :] = a_cur * a_prev
            db[lo:lo + SCAN_ROWS, :] = a_cur * b_prev + b_cur
    h = bufs[n_steps % 2][1][SCAN_PAD:, :]
    o_ref[0] = h * jax.nn.gelu(g_ref[0])


def _rglru(xr, gr, conv_w, conv_b, wa, ba, wx, bx, lam):
    b, s, c = xr.shape
    assert s <= 2 * SCAN_PAD and s % SCAN_ROWS == 0
    tok = pl.BlockSpec((1, s, c), lambda bi: (bi, 0, 0))
    vec = lambda n: pl.BlockSpec((n, c), lambda bi: (0, 0))
    return pl.pallas_call(
        _lru_body,
        grid=(b,),
        in_specs=[tok, tok, vec(CONV_WIDTH), vec(1), vec(c), vec(1), vec(c), vec(1), vec(1)],
        out_specs=tok,
        out_shape=jax.ShapeDtypeStruct((b, s, c), F32),
        scratch_shapes=[pltpu.VMEM((SCAN_PAD + s, c), F32)] * 4,
        compiler_params=_cparams(("parallel",)),
        name="rglru",
    )(xr, gr, conv_w, conv_b, wa, ba, wx, bx, lam)


def _mlstm_body(q_ref, v_ref, o_ref_in, kt_ref, gcol_ref, grow_ref, bcol_ref, brow_ref, tri_ref, out_ref):
    nc = q_ref.shape[2]
    L = ML_CHUNK
    ones_col = jnp.where(lax.broadcasted_iota(jnp.int32, (L, HEAD_DIM), 1) == 0, 1.0, 0.0)
    rr = lax.broadcasted_iota(jnp.int32, (L, L), 0)
    cc = lax.broadcasted_iota(jnp.int32, (L, L), 1)
    causal = cc <= rr
    srow = lax.broadcasted_iota(jnp.int32, (L, LANES), 0)

    def chunk(c, carry):
        gcol = gcol_ref[0, c] + bcol_ref[...]
        lf_col = jax.nn.log_sigmoid(gcol)
        a_col = lf_col
        for k in range(int(np.log2(L))):
            d = 1 << k
            a_col = a_col + jnp.where(srow >= d, pltpu.roll(a_col, d, axis=0), 0.0)
        grow = grow_ref[0, c] + brow_ref[...]
        lf_row = jax.nn.log_sigmoid(grow)
        a_rows = _dot_exact01(lf_row, tri_ref[...])
        new_carry = []
        outs = []
        for h in range(ML_HEADS):
            c_aug, m_prev = carry[h]
            q = q_ref[0, h, c].astype(BF16)
            kt = kt_ref[0, h, c]
            v = v_ref[0, h, c]
            v_aug = jnp.concatenate([v, ones_col], axis=1).astype(BF16)
            a_c = a_col[:, GATE_F + h:GATE_F + h + 1]
            a_r = jnp.broadcast_to(a_rows[ML_HEADS + h:ML_HEADS + h + 1, :], (SUBLANES, L))
            ig_r = jnp.broadcast_to(grow[h:h + 1, :], (SUBLANES, L))
            a_end = a_r[:, L - 1:L]
            tall = lambda s8: jnp.concatenate([s8] * (L // SUBLANES), axis=0)

            rhs = jnp.concatenate([c_aug.astype(BF16), kt.astype(BF16)], axis=1)
            q_both = jnp.dot(q, rhs, preferred_element_type=F32)
            q_state = q_both[:, :2 * HEAD_DIM]
            qk = q_both[:, 2 * HEAD_DIM:] * HEAD_DIM ** -0.5
            dmat = jnp.where(causal, a_c + (ig_r - a_r)[0:1, :], NEG)
            m_inter = a_c + tall(m_prev)
            m = jnp.maximum(m_inter, jnp.max(dmat, axis=1, keepdims=True))
            inter = jnp.exp(m_inter - m)
            w = jnp.exp(dmat - m) * qk
            num = inter * q_state + jnp.dot(w.astype(BF16), v_aug, preferred_element_type=F32)
            den = num[:, HEAD_DIM:HEAD_DIM + 1]
            hval = num[:, :HEAD_DIM] / jnp.maximum(jnp.abs(den), jnp.exp(-m))
            outs.append(jax.nn.sigmoid(o_ref_in[0, h, c]) * hval)

            w_end = a_end - a_r + ig_r
            m_new = jnp.maximum(a_end + m_prev, jnp.max(w_end, axis=1, keepdims=True))
            decay = jnp.exp(a_end + m_prev - m_new)
            wk = jnp.exp(w_end - m_new) * HEAD_DIM ** -0.5
            kw = (kt * wk[0:1, :]).astype(BF16)
            c_new = tall(decay) * c_aug + jnp.dot(kw, v_aug, preferred_element_type=F32)
            new_carry.append((c_new, m_new))
        out_ref[0, c] = jnp.concatenate(outs, axis=1)
        return tuple(new_carry)

    init = tuple((jnp.zeros((HEAD_DIM, 2 * HEAD_DIM), F32), jnp.zeros((SUBLANES, 1), F32)) for _ in range(ML_HEADS))
    lax.fori_loop(0, nc, chunk, init, unroll=2)


def _mlstm(heads_c, kt, gates_c, gt, bcol, brow, tri):
    b, _, nc, L, _ = heads_c.shape
    slot_spec = lambda slot: pl.BlockSpec((1, ML_HEADS, nc, L, HEAD_DIM), lambda bi: (bi, slot // ML_HEADS, 0, 0, 0))
    return pl.pallas_call(
        _mlstm_body,
        grid=(b,),
        in_specs=[
            slot_spec(SLOT_QM), slot_spec(SLOT_VM), slot_spec(SLOT_OM),
            pl.BlockSpec((1, ML_HEADS, nc, HEAD_DIM, L), lambda bi: (bi, 0, 0, 0, 0)),
            pl.BlockSpec((1, nc, L, LANES), lambda bi: (bi, 0, 0, 0)),
            pl.BlockSpec((1, nc, 2 * ML_HEADS, L), lambda bi: (bi, 0, 0, 0)),
            pl.BlockSpec((1, LANES), lambda bi: (0, 0)),
            pl.BlockSpec((2 * ML_HEADS, 1), lambda bi: (0, 0)),
            pl.BlockSpec((L, L), lambda bi: (0, 0)),
        ],
        out_specs=pl.BlockSpec((1, nc, L, ML_WIDTH), lambda bi: (bi, 0, 0, 0)),
        out_shape=jax.ShapeDtypeStruct((b, nc, L, ML_WIDTH), F32),
        compiler_params=_cparams(("parallel",)),
        name="mlstm",
    )(heads_c, heads_c, heads_c, kt, gates_c, gt, bcol, brow, tri)


def _outproj_body(h_ref, on_ref, ol_ref, om_ref, gn_ref, ind_ref, indt_ref, w_ref, o_ref):
    acc = h_ref[...]
    col = 0
    for x_ref in (on_ref, ol_ref, om_ref):
        x = x_ref[...]
        width = x.shape[1]
        ind = ind_ref[col:col + width, :]
        ms = jnp.dot((x * x).astype(BF16), ind, preferred_element_type=F32) * (1.0 / HEAD_DIM)
        r_hi, r_mid, _ = _split3(lax.rsqrt(ms + EPS))
        indt = indt_ref[:, col:col + width]
        scale = (jnp.dot(r_hi, indt, preferred_element_type=F32)
                 + jnp.dot(r_mid, indt, preferred_element_type=F32))
        xn = (x * scale * gn_ref[:, col:col + width]).astype(BF16)
        acc = acc + jnp.dot(xn, w_ref[0, col:col + width, :], preferred_element_type=F32)
        col += width
    o_ref[...] = acc


def _out_proj(h, o_nsa, o_lru, o_ml, gain, ind, indt, w_out, layer, *, tm=512):
    m, d = h.shape
    tok = lambda w: pl.BlockSpec((tm, w), lambda i: (i, 0))
    return pl.pallas_call(
        _outproj_body,
        grid=(m // tm,),
        in_specs=[
            tok(d), tok(o_nsa.shape[1]), tok(o_lru.shape[1]), tok(o_ml.shape[1]),
            pl.BlockSpec((1, d), lambda i: (0, 0)),
            pl.BlockSpec((d, LANES), lambda i: (0, 0)),
            pl.BlockSpec((LANES, d), lambda i: (0, 0)),
            pl.BlockSpec((1, d, d), lambda i: (layer, 0, 0)),
        ],
        out_specs=tok(d),
        out_shape=jax.ShapeDtypeStruct((m, d), F32),
        compiler_params=_cparams(("parallel",)),
        name="out_proj",
    )(h, o_nsa, o_lru, o_ml, gain, ind, indt, w_out)


def _pack_w_in(w_in):
    cols = np.concatenate([
        np.arange(_OFF_Q, _OFF_G),
        np.arange(_OFF_QM, _OFF_IM),
        np.arange(_OFF_OM, D_IN),
        np.arange(_OFF_XR, _OFF_QM),
        np.arange(_OFF_G, _OFF_XR),
        np.arange(_OFF_IM, _OFF_OM),
    ])
    main = w_in[:, :, cols]
    pad = LANES - (NSA_HEADS * 3 + 2 * ML_HEADS)
    main = jnp.pad(main, ((0, 0), (0, 0), (0, pad))).astype(BF16)
    off_vs = _OFF_KV + 3 * NSA_KV_WIDTH
    off_vw = _OFF_KV + 5 * NSA_KV_WIDTH
    t_cols = np.concatenate([np.arange(_OFF_KM, _OFF_VM), np.arange(_OFF_IM, _OFF_OM), np.arange(_OFF_G, _OFF_XR),
                             np.arange(off_vs, off_vs + NSA_KV_WIDTH), np.arange(off_vw, off_vw + NSA_KV_WIDTH)])
    w_t = jnp.swapaxes(w_in[:, :, t_cols], 1, 2).astype(BF16)
    return main, w_t


def _block_diag(w):
    depth, n, c, _ = w.shape
    eye = jnp.eye(n, dtype=w.dtype)
    return jnp.einsum("lncd,nm->lncmd", w, eye).reshape(depth, n * c, n * c)


def _constants(s):
    n_cmp_pad = s // CMP_STRIDE
    n = np.arange(n_cmp_pad)
    j = np.arange(LANES)
    cmp_start = n * CMP_STRIDE
    cmp_end = cmp_start + CMP_BLOCK - 1
    slc_start = j * SEL_BLOCK
    n_cmp = (s - CMP_BLOCK) // CMP_STRIDE + 1
    ov = ((cmp_start[:, None] <= slc_start[None, :] + SEL_BLOCK - 1) & (cmp_end[:, None] >= slc_start[None, :])
          & (n[:, None] < n_cmp) & (j[None, :] < s // SEL_BLOCK))
    key_block = np.arange(s) // SEL_BLOCK
    ext = np.where(key_block[:, None] == j[None, :], NEG, 0.0)
    tri = np.triu(np.ones((ML_CHUNK, ML_CHUNK)))
    heads = np.arange(D_MODEL) // HEAD_DIM
    ind = (heads[:, None] == j[None, :])
    slope = 2.0 ** (-8.0 * np.arange(1, NSA_HEADS + 1) / NSA_HEADS) * LOG2E
    parts = jnp.stack(_split3(jnp.asarray(slope, F32)), axis=1).astype(F32)
    cols = jnp.concatenate([parts, parts * float(LANES)], axis=1)
    cols = jnp.pad(cols, ((0, 0), (0, HEAD_DIM - 2 * N_IDX)))
    sl = jnp.repeat(cols, TQ, axis=0).reshape(NSA_GROUPS, NSA_HPG * TQ, HEAD_DIM)
    return dict(ovt=jnp.asarray(ov.T, BF16), ext=jnp.asarray(ext, BF16), tri=jnp.asarray(tri, BF16),
                ind=jnp.asarray(ind, BF16), indt=jnp.asarray(ind.T, BF16), sl=sl)


def _mixers(h3, layer, p):
    b, s, d = h3.shape
    heads, xr, gr, gates, kt, gt, gn, vt = _in_proj(h3, p["mix_norm"][layer][None], p["w_main"], p["w_t"], layer)

    kc_aug, vct = _compress(heads, p["cmp_pos"][layer], p["cmp_w1"][layer], p["cmp_w2k"][layer], p["cmp_w2vt"][layer])
    o_nsa = _nsa(heads, kc_aug, vct, vt, gn, p["sl"], p["ovt"], p["ext"])

    o_lru = _rglru(xr, gr, p["conv_w"][layer], p["conv_b"][layer][None], p["wa"][layer], p["ba"][layer][None],
                   p["wx"][layer], p["bx"][layer][None], p["lam"][layer][None])

    nc = s // ML_CHUNK
    heads_c = heads.reshape(b, N_SLOTS, nc, ML_CHUNK, HEAD_DIM)
    gates_c = gates.reshape(b, nc, ML_CHUNK, LANES)
    o_ml = _mlstm(heads_c, kt, gates_c, gt, p["ml_bcol"][layer][None], p["ml_brow"][layer][:, None], p["tri"])
    m = b * s
    return o_nsa.reshape(m, NSA_WIDTH), o_lru.reshape(m, LRU_WIDTH), o_ml.reshape(m, ML_WIDTH)


def kernel(x, ffn1_norm, ffn1_w1, ffn1_w3, ffn1_w2, mix_norm, w_in, nsa_cmp_pos_k, nsa_cmp_w1_k, nsa_cmp_w2_k, nsa_cmp_pos_v, nsa_cmp_w1_v, nsa_cmp_w2_v, lru_conv_w, lru_conv_b, lru_w_a, lru_b_a, lru_w_x, lru_b_x, lru_lambda, ml_b_i, ml_b_f, head_norm, w_out, ffn2_norm, ffn2_w1, ffn2_w3, ffn2_w2, final_norm):
    b, s, d = x.shape
    m = b * s
    depth = w_in.shape[0]
    consts = _constants(s)
    ind, indt = consts["ind"], consts["indt"]
    w_main, w_t = _pack_w_in(w_in)
    zeros_gate = jnp.zeros((depth, GATE_I), F32)
    bias_lanes = jnp.concatenate([zeros_gate, ml_b_i, ml_b_f], axis=1)
    p = {
        **consts,
        "mix_norm": mix_norm, "w_main": w_main, "w_t": w_t,
        "cmp_pos": jnp.stack([nsa_cmp_pos_k, nsa_cmp_pos_v], axis=1),
        "cmp_w1": jnp.stack([nsa_cmp_w1_k, nsa_cmp_w1_v], axis=1).astype(BF16),
        "cmp_w2k": nsa_cmp_w2_k.astype(BF16),
        "cmp_w2vt": jnp.swapaxes(nsa_cmp_w2_v, 1, 2).astype(BF16),
        "conv_w": lru_conv_w, "conv_b": lru_conv_b,
        "wa": _block_diag(lru_w_a).astype(BF16), "ba": lru_b_a.reshape(depth, LRU_WIDTH),
        "wx": _block_diag(lru_w_x).astype(BF16), "bx": lru_b_x.reshape(depth, LRU_WIDTH),
        "lam": lru_lambda,
        "ml_bcol": jnp.pad(bias_lanes, ((0, 0), (0, LANES - bias_lanes.shape[1]))),
        "ml_brow": jnp.concatenate([ml_b_i, ml_b_f], axis=1),
    }
    f1 = [w.astype(BF16) for w in (ffn1_w1, ffn1_w3, ffn1_w2)]
    f2 = [w.astype(BF16) for w in (ffn2_w1, ffn2_w3, ffn2_w2)]
    w_out_b = w_out.astype(BF16)
    gain_heads = head_norm.reshape(depth, 1, d)

    h = x.reshape(m, d)
    for layer in range(depth):
        h = _ffn(h, ffn1_norm[layer][None], *f1, layer)
        o_nsa, o_lru, o_ml = _mixers(h.reshape(b, s, d), layer, p)
        h = _out_proj(h, o_nsa, o_lru, o_ml, gain_heads[layer], ind, indt, w_out_b, layer)
        last = layer == depth - 1
        h = _ffn(h, ffn2_norm[layer][None], *f2, layer, final_norm[None] if last else None)
    return h.reshape(b, s, d)
```

```python
import functools

import jax
import jax.numpy as jnp
import numpy as np
from jax import lax
from jax.experimental import pallas as pl
from jax.experimental.pallas import tpu as pltpu

F32 = jnp.float32
BF16 = jnp.bfloat16

D_MODEL = 1024
DEPTH = 4
HEAD_DIM = 64
NSA_HEADS = 8
NSA_GROUPS = 2
NSA_HPG = NSA_HEADS // NSA_GROUPS
NSA_WIDTH = NSA_HEADS * HEAD_DIM
NSA_KV_WIDTH = NSA_GROUPS * HEAD_DIM
CMP_BLOCK = 32
CMP_STRIDE = 16
CMP_HIDDEN = 256
SEL_BLOCK = 64
SEL_TOPK = 16
SEL_LOCAL = 2
WINDOW = 512
FORCE_BONUS = 1.0e4
LRU_BLOCKS = 4
LRU_WIDTH = LRU_BLOCKS * HEAD_DIM
CONV_WIDTH = 4
LRU_C = 8.0
ML_HEADS = 4
ML_WIDTH = ML_HEADS * HEAD_DIM
ML_CHUNK = 64
D_FF = 2816
EPS = 1e-6
NEG = -1e30

LANES = 128
SUBLANES = 8
VMEM_LIMIT = 52 * 1024 * 1024

_OFF_Q = 0
_OFF_KV = NSA_WIDTH
_OFF_G = _OFF_KV + 6 * NSA_KV_WIDTH
_OFF_XR = _OFF_G + NSA_HEADS * 3
_OFF_GR = _OFF_XR + LRU_WIDTH
_OFF_QM = _OFF_GR + LRU_WIDTH
_OFF_KM = _OFF_QM + ML_WIDTH
_OFF_VM = _OFF_KM + ML_WIDTH
_OFF_IM = _OFF_VM + ML_WIDTH
_OFF_FM = _OFF_IM + ML_HEADS
_OFF_OM = _OFF_FM + ML_HEADS
D_IN = _OFF_OM + ML_WIDTH

N_SLOTS = 36
SLOT_Q, SLOT_KC, SLOT_VC, SLOT_KS, SLOT_VS, SLOT_KW, SLOT_VW = 0, 8, 10, 12, 14, 16, 18
SLOT_QM, SLOT_KM, SLOT_VM, SLOT_OM = 20, 24, 28, 32
GATE_I, GATE_F = NSA_HEADS * 3, NSA_HEADS * 3 + ML_HEADS
T_GATE = ML_WIDTH
T_NSAG = T_GATE + 2 * ML_HEADS
T_VS = T_NSAG + NSA_HEADS * 3
T_VW = T_VS + NSA_KV_WIDTH
N_T_ROWS = T_VW + NSA_KV_WIDTH
LOG2E = 1.4426950408889634


def _cparams(sem):
    return pltpu.CompilerParams(dimension_semantics=sem, vmem_limit_bytes=VMEM_LIMIT)


def _rms(x, g):
    return x * lax.rsqrt(jnp.mean(x * x, axis=-1, keepdims=True) + EPS) * g


def _split3(x):
    hi = x.astype(BF16)
    r1 = x - hi.astype(F32)
    mid = r1.astype(BF16)
    lo = (r1 - mid.astype(F32)).astype(BF16)
    return hi, mid, lo


def _dot_exact01(x, m01):
    hi, mid, lo = _split3(x)
    d = lambda a: jnp.dot(a, m01, preferred_element_type=F32)
    return d(hi) + d(mid) + d(lo)


def _ffn_body(h_ref, g_ref, w1_ref, w3_ref, w2_ref, fg_ref, o_ref, xn_ref, acc_ref, *, n_f, final):
    j = pl.program_id(1)

    @pl.when(j == 0)
    def _():
        xn_ref[...] = _rms(h_ref[...], g_ref[...]).astype(BF16)

    xn = xn_ref[...]
    a = jnp.dot(xn, w1_ref[0], preferred_element_type=F32)
    b = jnp.dot(xn, w3_ref[0], preferred_element_type=F32)
    gated = (a * jax.nn.sigmoid(a) * b).astype(BF16)
    part = jnp.dot(gated, w2_ref[0], preferred_element_type=F32)

    @pl.when(j == 0)
    def _():
        acc_ref[...] = part

    @pl.when(j > 0)
    def _():
        acc_ref[...] += part

    @pl.when(j == n_f - 1)
    def _():
        y = h_ref[...] + 0.5 * acc_ref[...]
        if final:
            y = _rms(y, fg_ref[...])
        o_ref[...] = y


def _ffn(h, gain, w1, w3, w2, layer, final_gain=None, *, tm=512, tf=1408):
    m, d = h.shape
    n_f = D_FF // tf
    final = final_gain is not None
    fg = final_gain if final else gain
    body = functools.partial(_ffn_body, n_f=n_f, final=final)
    return pl.pallas_call(
        body,
        grid=(m // tm, n_f),
        in_specs=[
            pl.BlockSpec((tm, d), lambda i, j: (i, 0)),
            pl.BlockSpec((1, d), lambda i, j: (0, 0)),
            pl.BlockSpec((1, d, tf), lambda i, j: (layer, 0, j)),
            pl.BlockSpec((1, d, tf), lambda i, j: (layer, 0, j)),
            pl.BlockSpec((1, tf, d), lambda i, j: (layer, j, 0)),
            pl.BlockSpec((1, d), lambda i, j: (0, 0)),
        ],
        out_specs=pl.BlockSpec((tm, d), lambda i, j: (i, 0)),
        out_shape=jax.ShapeDtypeStruct((m, d), F32),
        scratch_shapes=[pltpu.VMEM((tm, d), BF16), pltpu.VMEM((tm, d), F32)],
        compiler_params=_cparams(("parallel", "arbitrary")),
        name="ffn",
    )(h, gain, w1, w3, w2, fg)


def _inproj_body(h_ref, g_ref, w_ref, wt_ref, heads_ref, xr_ref, gr_ref, gates_ref, kt_ref, gt_ref, gn_ref, vt_ref,
                 *, tm):
    xn = _rms(h_ref[0], g_ref[...]).astype(BF16)
    per = 2 * LANES // HEAD_DIM
    for c in range(N_SLOTS // per):
        u = jnp.dot(xn, w_ref[0, :, c * 2 * LANES:(c + 1) * 2 * LANES], preferred_element_type=F32)
        for i in range(per):
            heads_ref[0, c * per + i] = u[:, i * HEAD_DIM:(i + 1) * HEAD_DIM]
    base = N_SLOTS * HEAD_DIM
    xr_ref[0] = jnp.dot(xn, w_ref[0, :, base:base + LRU_WIDTH], preferred_element_type=F32)
    gr_ref[0] = jnp.dot(xn, w_ref[0, :, base + LRU_WIDTH:base + 2 * LRU_WIDTH], preferred_element_type=F32)
    gates_ref[0] = jnp.dot(xn, w_ref[0, :, base + 2 * LRU_WIDTH:], preferred_element_type=F32)
    ut = lax.dot_general(wt_ref[0], xn, (((1,), (1,)), ((), ())), preferred_element_type=F32)
    for cc in range(tm // ML_CHUNK):
        cols = slice(cc * ML_CHUNK, (cc + 1) * ML_CHUNK)
        for hh in range(ML_HEADS):
            kt_ref[0, hh, cc] = ut[hh * HEAD_DIM:(hh + 1) * HEAD_DIM, cols]
        gt_ref[0, cc] = ut[T_GATE:T_NSAG, cols]
    gn_ref[0] = ut[T_NSAG:T_VS, :]
    for j in range(2 * NSA_GROUPS):
        vt_ref[0, j] = ut[T_VS + j * HEAD_DIM:T_VS + (j + 1) * HEAD_DIM, :]


def _in_proj(h3, gain, w_main, w_t, layer, *, tm=512):
    b, s, d = h3.shape
    n_main = w_main.shape[-1]
    nc = s // ML_CHUNK
    body = functools.partial(_inproj_body, tm=tm)
    return pl.pallas_call(
        body,
        grid=(b, s // tm),
        in_specs=[
            pl.BlockSpec((1, tm, d), lambda bi, ti: (bi, ti, 0)),
            pl.BlockSpec((1, d), lambda bi, ti: (0, 0)),
            pl.BlockSpec((1, d, n_main), lambda bi, ti: (layer, 0, 0)),
            pl.BlockSpec((1, N_T_ROWS, d), lambda bi, ti: (layer, 0, 0)),
        ],
        out_specs=[
            pl.BlockSpec((1, N_SLOTS, tm, HEAD_DIM), lambda bi, ti: (bi, 0, ti, 0)),
            pl.BlockSpec((1, tm, LRU_WIDTH), lambda bi, ti: (bi, ti, 0)),
            pl.BlockSpec((1, tm, LRU_WIDTH), lambda bi, ti: (bi, ti, 0)),
            pl.BlockSpec((1, tm, LANES), lambda bi, ti: (bi, ti, 0)),
            pl.BlockSpec((1, ML_HEADS, tm // ML_CHUNK, HEAD_DIM, ML_CHUNK), lambda bi, ti: (bi, 0, ti, 0, 0)),
            pl.BlockSpec((1, tm // ML_CHUNK, 2 * ML_HEADS, ML_CHUNK), lambda bi, ti: (bi, ti, 0, 0)),
            pl.BlockSpec((1, NSA_HEADS * 3, tm), lambda bi, ti: (bi, 0, ti)),
            pl.BlockSpec((1, 2 * NSA_GROUPS, HEAD_DIM, tm), lambda bi, ti: (bi, 0, 0, ti)),
        ],
        out_shape=[
            jax.ShapeDtypeStruct((b, N_SLOTS, s, HEAD_DIM), F32),
            jax.ShapeDtypeStruct((b, s, LRU_WIDTH), F32),
            jax.ShapeDtypeStruct((b, s, LRU_WIDTH), F32),
            jax.ShapeDtypeStruct((b, s, LANES), F32),
            jax.ShapeDtypeStruct((b, ML_HEADS, nc, HEAD_DIM, ML_CHUNK), F32),
            jax.ShapeDtypeStruct((b, nc, 2 * ML_HEADS, ML_CHUNK), F32),
            jax.ShapeDtypeStruct((b, NSA_HEADS * 3, s), F32),
            jax.ShapeDtypeStruct((b, 2 * NSA_GROUPS, HEAD_DIM, s), F32),
        ],
        compiler_params=_cparams(("parallel", "parallel")),
        name="in_proj",
    )(h3, gain, w_main, w_t)


def _compress_hidden(z_ref, pos_ref, w1_ref):
    n = z_ref.shape[2] // CMP_STRIDE
    top = bot = None
    for c in range(CMP_STRIDE):
        zc = z_ref[0, 0, pl.ds(c, n, stride=CMP_STRIDE), :]
        for half in range(2):
            off = half * CMP_STRIDE + c
            x = (zc + pos_ref[0, off:off + 1, :]).astype(BF16)
            part = jnp.dot(x, w1_ref[0, off * HEAD_DIM:(off + 1) * HEAD_DIM, :], preferred_element_type=F32)
            if half == 0:
                top = part if top is None else top + part
            else:
                bot = part if bot is None else bot + part
    hid = top + pltpu.roll(bot, n - 1, axis=0)
    return (hid * jax.nn.sigmoid(hid)).astype(BF16)


def _compress_body(zk_ref, zv_ref, pos_ref, w1_ref, w2k_ref, w2vt_ref, kc_ref, vct_ref):
    hk = _compress_hidden(zk_ref, pos_ref.at[0:1], w1_ref.at[0:1])
    hv = _compress_hidden(zv_ref, pos_ref.at[1:2], w1_ref.at[1:2])
    kc = jnp.dot(hk, w2k_ref[...], preferred_element_type=F32)
    n = kc.shape[0]
    blk_start = lax.broadcasted_iota(jnp.int32, (n, HEAD_DIM), 0) * CMP_STRIDE
    kc_ref[0, 0] = jnp.concatenate([kc, _position_columns(blk_start, HEAD_DIM)], axis=1).astype(BF16)
    vct = lax.dot_general(w2vt_ref[...], hv, (((1,), (1,)), ((), ())), preferred_element_type=F32)
    vct_ref[0, 0] = vct.astype(BF16)


def _compress(heads, pos, w1, w2k, w2vt):
    b, _, s, _ = heads.shape
    n = s // CMP_STRIDE
    slot = lambda first: pl.BlockSpec((1, 1, s, HEAD_DIM), lambda bi, g: (bi, first + g, 0, 0))
    whole = lambda a: pl.BlockSpec(a.shape, lambda bi, g: (0,) * a.ndim)
    return pl.pallas_call(
        _compress_body,
        grid=(b, NSA_GROUPS),
        in_specs=[slot(SLOT_KC), slot(SLOT_VC), whole(pos), whole(w1), whole(w2k), whole(w2vt)],
        out_specs=[
            pl.BlockSpec((1, 1, n, 2 * HEAD_DIM), lambda bi, g: (bi, g, 0, 0)),
            pl.BlockSpec((1, 1, HEAD_DIM, n), lambda bi, g: (bi, g, 0, 0)),
        ],
        out_shape=[
            jax.ShapeDtypeStruct((b, NSA_GROUPS, n, 2 * HEAD_DIM), BF16),
            jax.ShapeDtypeStruct((b, NSA_GROUPS, HEAD_DIM, n), BF16),
        ],
        compiler_params=_cparams(("parallel", "parallel")),
        name="compress",
    )(heads, heads, pos, w1, w2k, w2vt)


TQ = 256
N_SEL = 32


N_IDX = 3
SEL_KEYS = 512
WIN_KEYS = WINDOW + TQ
V_ROWS = HEAD_DIM + 16


def _position_columns(pos, width):
    lane = lax.broadcasted_iota(jnp.int32, (pos.shape[0], width), 1)
    lo = jnp.bitwise_and(pos, LANES - 1).astype(F32)
    hi = jnp.right_shift(pos, int(np.log2(LANES))).astype(F32)
    return jnp.where(lane < N_IDX, lo, jnp.where(lane < 2 * N_IDX, hi, 0.0))


def _nsa_body(q_ref, sl_ref, kc_ref, vct_ref, ks_ref, kw_ref, vst_ref, vwt_ref, gn_ref, ovt_ref, ext_ref,
              o_ref, ksa_sc, kwa_sc, vs_sc, vw_sc):
    g = pl.program_id(1)
    qi = pl.program_id(2)
    s_len = ks_ref.shape[2]
    r_cols = NSA_HPG * TQ
    nt = (((1,), (1,)), ((), ()))

    @pl.when(qi == 0)
    def _():
        for c in range(s_len // TQ):
            rows = slice(c * TQ, (c + 1) * TQ)
            pos = c * TQ + lax.broadcasted_iota(jnp.int32, (TQ, HEAD_DIM), 0)
            idx = _position_columns(pos, HEAD_DIM)
            ksa_sc[rows, :] = jnp.concatenate([ks_ref[0, 0, rows, :], idx], axis=1).astype(BF16)
            kwa_sc[rows, :] = jnp.concatenate([kw_ref[0, 0, rows, :], idx], axis=1).astype(BF16)
        ones_rows = jnp.where(lax.broadcasted_iota(jnp.int32, (V_ROWS - HEAD_DIM, s_len), 0) == 0, 1.0, 0.0)
        vs_sc[...] = jnp.concatenate([vst_ref[0, 0], ones_rows], axis=0).astype(BF16)
        vw_sc[...] = jnp.concatenate([vwt_ref[0, 0], ones_rows], axis=0).astype(BF16)

    q = q_ref[0].reshape(r_cols, HEAD_DIM) * (HEAD_DIM ** -0.5 * LOG2E)
    q_aug = jnp.concatenate([q, sl_ref[0]], axis=1).astype(BF16)

    def tile4(x):
        return jnp.concatenate([x] * NSA_HPG, axis=1)

    def rel_pos(n_keys, start):
        k_loc = lax.broadcasted_iota(jnp.int32, (n_keys, TQ), 0)
        q_loc = lax.broadcasted_iota(jnp.int32, (n_keys, TQ), 1)
        return (q_loc - k_loc) + (qi * TQ - start)

    n_cmp = kc_ref.shape[2]
    k_loc = lax.broadcasted_iota(jnp.int32, (n_cmp, TQ), 0)
    q_loc = lax.broadcasted_iota(jnp.int32, (n_cmp, TQ), 1)
    s_c = lax.dot_general(kc_ref[0, 0], q_aug, nt, preferred_element_type=F32)
    ok_c = tile4(k_loc * CMP_STRIDE + (CMP_BLOCK - 1) <= qi * TQ + q_loc)
    s_c = jnp.where(ok_c, s_c, NEG)
    m_c = jnp.max(s_c, axis=0, keepdims=True)
    e_c = jnp.where(ok_c, jnp.exp2(s_c - m_c), 0.0)
    l_c = jnp.sum(e_c, axis=0, keepdims=True)
    p_c = e_c / jnp.where(l_c > 0.0, l_c, 1.0)
    o_c = jnp.dot(vct_ref[0, 0], p_c.astype(BF16), preferred_element_type=F32)

    p_sum = p_c[:, 0:TQ]
    for r in range(1, NSA_HPG):
        p_sum = p_sum + p_c[:, r * TQ:(r + 1) * TQ]
    ovt = ovt_ref[...]
    imp = sum(jnp.dot(ovt, part, preferred_element_type=F32) for part in _split3(p_sum))[:N_SEL]
    blk = lax.broadcasted_iota(jnp.int32, (N_SEL, TQ), 0)
    cur = jnp.right_shift(qi * TQ + lax.broadcasted_iota(jnp.int32, (N_SEL, TQ), 1), int(np.log2(SEL_BLOCK)))
    valid = blk <= cur
    forced = valid & ((blk == 0) | (blk > cur - SEL_LOCAL))
    score = jnp.where(valid, imp + jnp.where(forced, FORCE_BONUS, 0.0), NEG)
    rank = jnp.zeros((N_SEL, TQ), F32)
    for jp in range(N_SEL):
        row = score[jp:jp + 1, :]
        beats = (row > score) | ((row == score) & (blk > jp))
        rank = rank + jnp.where(beats, 1.0, 0.0)
    unsel = jnp.where(rank < float(SEL_TOPK), 0.0, 1.0)
    unsel = jnp.concatenate([unsel, jnp.zeros((LANES - N_SEL, TQ), F32)], axis=0).astype(BF16)

    def attend(ka_sc, vt_sc, start, n_keys, carry, extra=None, keep=None):
        m, acc = carry
        s = lax.dot_general(ka_sc[pl.ds(start, n_keys), :], q_aug, nt, preferred_element_type=F32)
        if extra is not None:
            s = s + tile4(extra)
        if keep is not None:
            s = jnp.where(tile4(keep), s, NEG)
        m_new = jnp.maximum(m, jnp.max(s, axis=0, keepdims=True))
        p = jnp.exp2((s - m_new).astype(BF16))
        pv = jnp.dot(vt_sc[:, pl.ds(start, n_keys)], p, preferred_element_type=F32)
        return m_new, jnp.exp2(m - m_new) * acc + pv

    def normalise(acc):
        return acc[:HEAD_DIM] / acc[HEAD_DIM:HEAD_DIM + 1]

    init = (jnp.full((1, r_cols), NEG, F32), jnp.zeros((V_ROWS, r_cols), F32))

    def sel_step(j, carry, last, n_keys=SEL_KEYS):
        start = pl.multiple_of(j * SEL_KEYS, SEL_KEYS)
        bias = jnp.dot(ext_ref[pl.ds(start, n_keys), :], unsel, preferred_element_type=F32)
        keep = rel_pos(n_keys, start) >= 0 if last else None
        return attend(ksa_sc, vs_sc, start, n_keys, carry, extra=bias, keep=keep)

    n_full = (qi * TQ) // SEL_KEYS
    carry = lax.fori_loop(0, n_full, lambda j, c: sel_step(j, c, False), init)
    o_s = normalise(sel_step(n_full, carry, True)[1])

    w_start = pl.multiple_of(jnp.maximum(qi * TQ - WINDOW, 0), TQ)
    rel = rel_pos(WIN_KEYS, w_start)
    o_w = normalise(attend(kwa_sc, vw_sc, w_start, WIN_KEYS, init, keep=(rel >= 0) & (rel < WINDOW))[1])

    gates = jax.nn.sigmoid(gn_ref[0])
    outs = []
    for r in range(NSA_HPG):
        cols = slice(r * TQ, (r + 1) * TQ)
        acc = None
        for br, o_b in enumerate((o_c, o_s, o_w)):
            row_lo = r * 3 + br
            g0 = gates[row_lo:row_lo + 1, :]
            g1 = gates[NSA_HPG * 3 + row_lo:NSA_HPG * 3 + row_lo + 1, :]
            term = jnp.where(g == 0, g0, g1) * o_b[:, cols]
            acc = term if acc is None else acc + term
        outs.append(acc)
    o_ref[0] = jnp.concatenate(outs, axis=0).T


def _nsa(heads, kc_aug, vct, vt, gn, sl, ovt, ext):
    b, _, s, _ = heads.shape
    n_cmp = kc_aug.shape[2]
    assert n_cmp == LANES and s // SEL_BLOCK == N_SEL and s % SEL_KEYS == 0
    r_cols = NSA_HPG * TQ
    k_spec = lambda slot: pl.BlockSpec((1, 1, s, HEAD_DIM), lambda bi, g, qi: (bi, slot + g, 0, 0))
    vt_spec = lambda first: pl.BlockSpec((1, 1, HEAD_DIM, s), lambda bi, g, qi: (bi, first + g, 0, 0))
    return pl.pallas_call(
        _nsa_body,
        grid=(b, NSA_GROUPS, s // TQ),
        in_specs=[
            pl.BlockSpec((1, NSA_HPG, TQ, HEAD_DIM), lambda bi, g, qi: (bi, g, qi, 0)),
            pl.BlockSpec((1, r_cols, HEAD_DIM), lambda bi, g, qi: (g, 0, 0)),
            pl.BlockSpec((1, 1, n_cmp, 2 * HEAD_DIM), lambda bi, g, qi: (bi, g, 0, 0)),
            pl.BlockSpec((1, 1, HEAD_DIM, n_cmp), lambda bi, g, qi: (bi, g, 0, 0)),
            k_spec(SLOT_KS), k_spec(SLOT_KW), vt_spec(0), vt_spec(NSA_GROUPS),
            pl.BlockSpec((1, NSA_HEADS * 3, TQ), lambda bi, g, qi: (bi, 0, qi)),
            pl.BlockSpec((LANES, LANES), lambda bi, g, qi: (0, 0)),
            pl.BlockSpec((s, LANES), lambda bi, g, qi: (0, 0)),
        ],
        out_specs=pl.BlockSpec((1, TQ, NSA_HPG * HEAD_DIM), lambda bi, g, qi: (bi, qi, g)),
        out_shape=jax.ShapeDtypeStruct((b, s, NSA_WIDTH), F32),
        scratch_shapes=[
            pltpu.VMEM((s, 2 * HEAD_DIM), BF16),
            pltpu.VMEM((s, 2 * HEAD_DIM), BF16),
            pltpu.VMEM((V_ROWS, s), BF16),
            pltpu.VMEM((V_ROWS, s), BF16),
        ],
        compiler_params=_cparams(("arbitrary", "arbitrary", "arbitrary")),
        name="nsa",
    )(heads, sl, kc_aug, vct, heads, heads, vt, vt, gn, ovt, ext)


SCAN_PAD = 1024
SCAN_ROWS = 256


def _lru_body(x_ref, g_ref, cw_ref, cb_ref, wa_ref, ba_ref, wx_ref, bx_ref, lam_ref, o_ref,
              a0, b0, a1, b1):
    s = x_ref.shape[1]
    x = x_ref[0]
    row = lax.broadcasted_iota(jnp.int32, x.shape, 0)
    xc = cb_ref[...]
    for tap in range(CONV_WIDTH):
        back = CONV_WIDTH - 1 - tap
        xs = x if back == 0 else jnp.where(row >= back, pltpu.roll(x, back, axis=0), 0.0)
        xc = xc + xs * cw_ref[tap:tap + 1, :]
    xb = xc.astype(BF16)
    r = jax.nn.sigmoid(jnp.dot(xb, wa_ref[...], preferred_element_type=F32) + ba_ref[...])
    i = jax.nn.sigmoid(jnp.dot(xb, wx_ref[...], preferred_element_type=F32) + bx_ref[...])
    log_a = -LRU_C * r * jax.nn.softplus(-lam_ref[...])
    a = jnp.exp(log_a)
    u = jnp.sqrt(-jnp.tanh(log_a) * (a * a + 1.0)) * (i * xc)

    bufs = ((a0, b0), (a1, b1))
    for ab, bb in bufs:
        ab[0:SCAN_PAD, :] = jnp.ones((SCAN_PAD, x.shape[1]), F32)
        bb[0:SCAN_PAD, :] = jnp.zeros((SCAN_PAD, x.shape[1]), F32)
    a0[SCAN_PAD:, :] = a
    b0[SCAN_PAD:, :] = u
    n_steps = int(np.log2(s))
    for k in range(n_steps):
        d = 1 << k
        (sa, sb), (da, db) = bufs[k % 2], bufs[(k + 1) % 2]
        for c in range(s // SCAN_ROWS):
            lo = SCAN_PAD + c * SCAN_ROWS
            a_cur = sa[lo:lo + SCAN_ROWS, :]
            a_prev = sa[lo - d:lo - d + SCAN_ROWS, :]
            b_cur = sb[lo:lo + SCAN_ROWS, :]
            b_prev = sb[lo - d:lo - d + SCAN_ROWS, :]
            da[lo:lo + SCAN_ROWS, :] = a_cur * a_prev
            db[lo:lo + SCAN_ROWS, :] = a_cur * b_prev + b_cur
    h = bufs[n_steps % 2][1][SCAN_PAD:, :]
    o_ref[0] = h * jax.nn.gelu(g_ref[0])


def _rglru(xr, gr, conv_w, conv_b, wa, ba, wx, bx, lam):
    b, s, c = xr.shape
    assert s <= 2 * SCAN_PAD and s % SCAN_ROWS == 0
    tok = pl.BlockSpec((1, s, c), lambda bi: (bi, 0, 0))
    vec = lambda n: pl.BlockSpec((n, c), lambda bi: (0, 0))
    return pl.pallas_call(
        _lru_body,
        grid=(b,),
        in_specs=[tok, tok, vec(CONV_WIDTH), vec(1), vec(c), vec(1), vec(c), vec(1), vec(1)],
        out_specs=tok,
        out_shape=jax.ShapeDtypeStruct((b, s, c), F32),
        scratch_shapes=[pltpu.VMEM((SCAN_PAD + s, c), F32)] * 4,
        compiler_params=_cparams(("parallel",)),
        name="rglru",
    )(xr, gr, conv_w, conv_b, wa, ba, wx, bx, lam)


def _mlstm_body(q_ref, v_ref, o_ref_in, kt_ref, gcol_ref, grow_ref, bcol_ref, brow_ref, tri_ref, out_ref):
    nc = q_ref.shape[2]
    L = ML_CHUNK
    ones_col = jnp.where(lax.broadcasted_iota(jnp.int32, (L, HEAD_DIM), 1) == 0, 1.0, 0.0)
    rr = lax.broadcasted_iota(jnp.int32, (L, L), 0)
    cc = lax.broadcasted_iota(jnp.int32, (L, L), 1)
    causal = cc <= rr
    srow = lax.broadcasted_iota(jnp.int32, (L, LANES), 0)

    def chunk(c, carry):
        gcol = gcol_ref[0, c] + bcol_ref[...]
        lf_col = jax.nn.log_sigmoid(gcol)
        a_col = lf_col
        for k in range(int(np.log2(L))):
            d = 1 << k
            a_col = a_col + jnp.where(srow >= d, pltpu.roll(a_col, d, axis=0), 0.0)
        grow = grow_ref[0, c] + brow_ref[...]
        lf_row = jax.nn.log_sigmoid(grow)
        a_rows = _dot_exact01(lf_row, tri_ref[...])
        new_carry = []
        outs = []
        for h in range(ML_HEADS):
            c_aug, m_prev = carry[h]
            q = q_ref[0, h, c].astype(BF16)
            kt = kt_ref[0, h, c]
            v = v_ref[0, h, c]
            v_aug = jnp.concatenate([v, ones_col], axis=1).astype(BF16)
            a_c = a_col[:, GATE_F + h:GATE_F + h + 1]
            a_r = jnp.broadcast_to(a_rows[ML_HEADS + h:ML_HEADS + h + 1, :], (SUBLANES, L))
            ig_r = jnp.broadcast_to(grow[h:h + 1, :], (SUBLANES, L))
            a_end = a_r[:, L - 1:L]
            tall = lambda s8: jnp.concatenate([s8] * (L // SUBLANES), axis=0)

            rhs = jnp.concatenate([c_aug.astype(BF16), kt.astype(BF16)], axis=1)
            q_both = jnp.dot(q, rhs, preferred_element_type=F32)
            q_state = q_both[:, :2 * HEAD_DIM]
            qk = q_both[:, 2 * HEAD_DIM:] * HEAD_DIM ** -0.5
            dmat = jnp.where(causal, a_c + (ig_r - a_r)[0:1, :], NEG)
            m_inter = a_c + tall(m_prev)
            m = jnp.maximum(m_inter, jnp.max(dmat, axis=1, keepdims=True))
            inter = jnp.exp(m_inter - m)
            w = jnp.exp(dmat - m) * qk
            num = inter * q_state + jnp.dot(w.astype(BF16), v_aug, preferred_element_type=F32)
            den = num[:, HEAD_DIM:HEAD_DIM + 1]
            hval = num[:, :HEAD_DIM] / jnp.maximum(jnp.abs(den), jnp.exp(-m))
            outs.append(jax.nn.sigmoid(o_ref_in[0, h, c]) * hval)

            w_end = a_end - a_r + ig_r
            m_new = jnp.maximum(a_end + m_prev, jnp.max(w_end, axis=1, keepdims=True))
            decay = jnp.exp(a_end + m_prev - m_new)
            wk = jnp.exp(w_end - m_new) * HEAD_DIM ** -0.5
            kw = (kt * wk[0:1, :]).astype(BF16)
            c_new = tall(decay) * c_aug + jnp.dot(kw, v_aug, preferred_element_type=F32)
            new_carry.append((c_new, m_new))
        out_ref[0, c] = jnp.concatenate(outs, axis=1)
        return tuple(new_carry)

    init = tuple((jnp.zeros((HEAD_DIM, 2 * HEAD_DIM), F32), jnp.zeros((SUBLANES, 1), F32)) for _ in range(ML_HEADS))
    lax.fori_loop(0, nc, chunk, init, unroll=2)


def _mlstm(heads_c, kt, gates_c, gt, bcol, brow, tri):
    b, _, nc, L, _ = heads_c.shape
    slot_spec = lambda slot: pl.BlockSpec((1, ML_HEADS, nc, L, HEAD_DIM), lambda bi: (bi, slot // ML_HEADS, 0, 0, 0))
    return pl.pallas_call(
        _mlstm_body,
        grid=(b,),
        in_specs=[
            slot_spec(SLOT_QM), slot_spec(SLOT_VM), slot_spec(SLOT_OM),
            pl.BlockSpec((1, ML_HEADS, nc, HEAD_DIM, L), lambda bi: (bi, 0, 0, 0, 0)),
            pl.BlockSpec((1, nc, L, LANES), lambda bi: (bi, 0, 0, 0)),
            pl.BlockSpec((1, nc, 2 * ML_HEADS, L), lambda bi: (bi, 0, 0, 0)),
            pl.BlockSpec((1, LANES), lambda bi: (0, 0)),
            pl.BlockSpec((2 * ML_HEADS, 1), lambda bi: (0, 0)),
            pl.BlockSpec((L, L), lambda bi: (0, 0)),
        ],
        out_specs=pl.BlockSpec((1, nc, L, ML_WIDTH), lambda bi: (bi, 0, 0, 0)),
        out_shape=jax.ShapeDtypeStruct((b, nc, L, ML_WIDTH), F32),
        compiler_params=_cparams(("parallel",)),
        name="mlstm",
    )(heads_c, heads_c, heads_c, kt, gates_c, gt, bcol, brow, tri)


def _outproj_body(h_ref, on_ref, ol_ref, om_ref, gn_ref, ind_ref, indt_ref, w_ref, o_ref):
    acc = h_ref[...]
    col = 0
    for x_ref in (on_ref, ol_ref, om_ref):
        x = x_ref[...]
        width = x.shape[1]
        ind = ind_ref[col:col + width, :]
        ms = jnp.dot((x * x).astype(BF16), ind, preferred_element_type=F32) * (1.0 / HEAD_DIM)
        r_hi, r_mid, _ = _split3(lax.rsqrt(ms + EPS))
        indt = indt_ref[:, col:col + width]
        scale = (jnp.dot(r_hi, indt, preferred_element_type=F32)
                 + jnp.dot(r_mid, indt, preferred_element_type=F32))
        xn = (x * scale * gn_ref[:, col:col + width]).astype(BF16)
        acc = acc + jnp.dot(xn, w_ref[0, col:col + width, :], preferred_element_type=F32)
        col += width
    o_ref[...] = acc


def _out_proj(h, o_nsa, o_lru, o_ml, gain, ind, indt, w_out, layer, *, tm=512):
    m, d = h.shape
    tok = lambda w: pl.BlockSpec((tm, w), lambda i: (i, 0))
    return pl.pallas_call(
        _outproj_body,
        grid=(m // tm,),
        in_specs=[
            tok(d), tok(o_nsa.shape[1]), tok(o_lru.shape[1]), tok(o_ml.shape[1]),
            pl.BlockSpec((1, d), lambda i: (0, 0)),
            pl.BlockSpec((d, LANES), lambda i: (0, 0)),
            pl.BlockSpec((LANES, d), lambda i: (0, 0)),
            pl.BlockSpec((1, d, d), lambda i: (layer, 0, 0)),
        ],
        out_specs=tok(d),
        out_shape=jax.ShapeDtypeStruct((m, d), F32),
        compiler_params=_cparams(("parallel",)),
        name="out_proj",
    )(h, o_nsa, o_lru, o_ml, gain, ind, indt, w_out)


def _pack_w_in(w_in):
    wb = w_in.astype(BF16)
    cols = lambda lo, hi: wb[:, :, lo:hi]
    pad = jnp.zeros(wb.shape[:2] + (LANES - (NSA_HEADS * 3 + 2 * ML_HEADS),), BF16)
    main = jnp.concatenate([
        cols(_OFF_Q, _OFF_G),
        cols(_OFF_QM, _OFF_IM),
        cols(_OFF_OM, D_IN),
        cols(_OFF_XR, _OFF_QM),
        cols(_OFF_G, _OFF_XR),
        cols(_OFF_IM, _OFF_OM),
        pad,
    ], axis=2)
    off_vs = _OFF_KV + 3 * NSA_KV_WIDTH
    off_vw = _OFF_KV + 5 * NSA_KV_WIDTH
    w_t = jnp.concatenate([cols(_OFF_KM, _OFF_VM), cols(_OFF_IM, _OFF_OM), cols(_OFF_G, _OFF_XR),
                           cols(off_vs, off_vs + NSA_KV_WIDTH), cols(off_vw, off_vw + NSA_KV_WIDTH)], axis=2)
    return main, jnp.swapaxes(w_t, 1, 2)


def _block_diag(w):
    depth, n, c, _ = w.shape
    eye = jnp.eye(n, dtype=w.dtype)
    return jnp.einsum("lncd,nm->lncmd", w, eye).reshape(depth, n * c, n * c)


def _constants(s):
    n_cmp_pad = s // CMP_STRIDE
    n = np.arange(n_cmp_pad)
    j = np.arange(LANES)
    cmp_start = n * CMP_STRIDE
    cmp_end = cmp_start + CMP_BLOCK - 1
    slc_start = j * SEL_BLOCK
    n_cmp = (s - CMP_BLOCK) // CMP_STRIDE + 1
    ov = ((cmp_start[:, None] <= slc_start[None, :] + SEL_BLOCK - 1) & (cmp_end[:, None] >= slc_start[None, :])
          & (n[:, None] < n_cmp) & (j[None, :] < s // SEL_BLOCK))
    key_block = np.arange(s) // SEL_BLOCK
    ext = np.where(key_block[:, None] == j[None, :], NEG, 0.0)
    tri = np.triu(np.ones((ML_CHUNK, ML_CHUNK)))
    heads = np.arange(D_MODEL) // HEAD_DIM
    ind = (heads[:, None] == j[None, :])
    slope = 2.0 ** (-8.0 * np.arange(1, NSA_HEADS + 1) / NSA_HEADS) * LOG2E
    parts = jnp.stack(_split3(jnp.asarray(slope, F32)), axis=1).astype(F32)
    cols = jnp.concatenate([parts, parts * float(LANES)], axis=1)
    cols = jnp.pad(cols, ((0, 0), (0, HEAD_DIM - 2 * N_IDX)))
    sl = jnp.repeat(cols, TQ, axis=0).reshape(NSA_GROUPS, NSA_HPG * TQ, HEAD_DIM)
    return dict(ovt=jnp.asarray(ov.T, BF16), ext=jnp.asarray(ext, BF16), tri=jnp.asarray(tri, BF16),
                ind=jnp.asarray(ind, BF16), indt=jnp.asarray(ind.T, BF16), sl=sl)


def _mixers(h3, layer, p):
    b, s, d = h3.shape
    heads, xr, gr, gates, kt, gt, gn, vt = _in_proj(h3, p["mix_norm"][layer][None], p["w_main"], p["w_t"], layer)

    kc_aug, vct = _compress(heads, p["cmp_pos"][layer], p["cmp_w1"][layer], p["cmp_w2k"][layer], p["cmp_w2vt"][layer])
    o_nsa = _nsa(heads, kc_aug, vct, vt, gn, p["sl"], p["ovt"], p["ext"])

    o_lru = _rglru(xr, gr, p["conv_w"][layer], p["conv_b"][layer][None], p["wa"][layer], p["ba"][layer][None],
                   p["wx"][layer], p["bx"][layer][None], p["lam"][layer][None])

    nc = s // ML_CHUNK
    heads_c = heads.reshape(b, N_SLOTS, nc, ML_CHUNK, HEAD_DIM)
    gates_c = gates.reshape(b, nc, ML_CHUNK, LANES)
    o_ml = _mlstm(heads_c, kt, gates_c, gt, p["ml_bcol"][layer][None], p["ml_brow"][layer][:, None], p["tri"])
    m = b * s
    return o_nsa.reshape(m, NSA_WIDTH), o_lru.reshape(m, LRU_WIDTH), o_ml.reshape(m, ML_WIDTH)


def kernel(x, ffn1_norm, ffn1_w1, ffn1_w3, ffn1_w2, mix_norm, w_in, nsa_cmp_pos_k, nsa_cmp_w1_k, nsa_cmp_w2_k, nsa_cmp_pos_v, nsa_cmp_w1_v, nsa_cmp_w2_v, lru_conv_w, lru_conv_b, lru_w_a, lru_b_a, lru_w_x, lru_b_x, lru_lambda, ml_b_i, ml_b_f, head_norm, w_out, ffn2_norm, ffn2_w1, ffn2_w3, ffn2_w2, final_norm):
    b, s, d = x.shape
    m = b * s
    depth = w_in.shape[0]
    consts = _constants(s)
    ind, indt = consts["ind"], consts["indt"]
    w_main, w_t = _pack_w_in(w_in)
    zeros_gate = jnp.zeros((depth, GATE_I), F32)
    bias_lanes = jnp.concatenate([zeros_gate, ml_b_i, ml_b_f], axis=1)
    p = {
        **consts,
        "mix_norm": mix_norm, "w_main": w_main, "w_t": w_t,
        "cmp_pos": jnp.stack([nsa_cmp_pos_k, nsa_cmp_pos_v], axis=1),
        "cmp_w1": jnp.stack([nsa_cmp_w1_k, nsa_cmp_w1_v], axis=1).astype(BF16),
        "cmp_w2k": nsa_cmp_w2_k.astype(BF16),
        "cmp_w2vt": jnp.swapaxes(nsa_cmp_w2_v, 1, 2).astype(BF16),
        "conv_w": lru_conv_w, "conv_b": lru_conv_b,
        "wa": _block_diag(lru_w_a).astype(BF16), "ba": lru_b_a.reshape(depth, LRU_WIDTH),
        "wx": _block_diag(lru_w_x).astype(BF16), "bx": lru_b_x.reshape(depth, LRU_WIDTH),
        "lam": lru_lambda,
        "ml_bcol": jnp.pad(bias_lanes, ((0, 0), (0, LANES - bias_lanes.shape[1]))),
        "ml_brow": jnp.concatenate([ml_b_i, ml_b_f], axis=1),
    }
    f1 = [w.astype(BF16) for w in (ffn1_w1, ffn1_w3, ffn1_w2)]
    f2 = [w.astype(BF16) for w in (ffn2_w1, ffn2_w3, ffn2_w2)]
    w_out_b = w_out.astype(BF16)
    gain_heads = head_norm.reshape(depth, 1, d)

    h = x.reshape(m, d)
    for layer in range(depth):
        h = _ffn(h, ffn1_norm[layer][None], *f1, layer)
        o_nsa, o_lru, o_ml = _mixers(h.reshape(b, s, d), layer, p)
        h = _out_proj(h, o_nsa, o_lru, o_ml, gain_heads[layer], ind, indt, w_out_b, layer)
        last = layer == depth - 1
        h = _ffn(h, ffn2_norm[layer][None], *f2, layer, final_norm[None] if last else None)
    return h.reshape(b, s, d)
```

```python
import functools

import jax
import jax.numpy as jnp
import numpy as np
from jax import lax
from jax.experimental import pallas as pl
from jax.experimental.pallas import tpu as pltpu

F32 = jnp.float32
BF16 = jnp.bfloat16

D_MODEL = 1024
DEPTH = 4
HEAD_DIM = 64
NSA_HEADS = 8
NSA_GROUPS = 2
NSA_HPG = NSA_HEADS // NSA_GROUPS
NSA_WIDTH = NSA_HEADS * HEAD_DIM
NSA_KV_WIDTH = NSA_GROUPS * HEAD_DIM
CMP_BLOCK = 32
CMP_STRIDE = 16
CMP_HIDDEN = 256
SEL_BLOCK = 64
SEL_TOPK = 16
SEL_LOCAL = 2
WINDOW = 512
FORCE_BONUS = 1.0e4
LRU_BLOCKS = 4
LRU_WIDTH = LRU_BLOCKS * HEAD_DIM
CONV_WIDTH = 4
LRU_C = 8.0
ML_HEADS = 4
ML_WIDTH = ML_HEADS * HEAD_DIM
ML_CHUNK = 64
D_FF = 2816
EPS = 1e-6
NEG = -1e30

LANES = 128
SUBLANES = 8
VMEM_LIMIT = 52 * 1024 * 1024

_OFF_Q = 0
_OFF_KV = NSA_WIDTH
_OFF_G = _OFF_KV + 6 * NSA_KV_WIDTH
_OFF_XR = _OFF_G + NSA_HEADS * 3
_OFF_GR = _OFF_XR + LRU_WIDTH
_OFF_QM = _OFF_GR + LRU_WIDTH
_OFF_KM = _OFF_QM + ML_WIDTH
_OFF_VM = _OFF_KM + ML_WIDTH
_OFF_IM = _OFF_VM + ML_WIDTH
_OFF_FM = _OFF_IM + ML_HEADS
_OFF_OM = _OFF_FM + ML_HEADS
D_IN = _OFF_OM + ML_WIDTH

N_SLOTS = 36
SLOT_Q, SLOT_KC, SLOT_VC, SLOT_KS, SLOT_VS, SLOT_KW, SLOT_VW = 0, 8, 10, 12, 14, 16, 18
SLOT_QM, SLOT_KM, SLOT_VM, SLOT_OM = 20, 24, 28, 32
GATE_I, GATE_F = NSA_HEADS * 3, NSA_HEADS * 3 + ML_HEADS
T_GATE = ML_WIDTH
T_NSAG = T_GATE + 2 * ML_HEADS
T_VS = T_NSAG + NSA_HEADS * 3
T_VW = T_VS + NSA_KV_WIDTH
N_T_ROWS = T_VW + NSA_KV_WIDTH
LOG2E = 1.4426950408889634


def _cparams(sem):
    return pltpu.CompilerParams(dimension_semantics=sem, vmem_limit_bytes=VMEM_LIMIT)


def _rms(x, g):
    return x * lax.rsqrt(jnp.mean(x * x, axis=-1, keepdims=True) + EPS) * g


def _split3(x):
    hi = x.astype(BF16)
    r1 = x - hi.astype(F32)
    mid = r1.astype(BF16)
    lo = (r1 - mid.astype(F32)).astype(BF16)
    return hi, mid, lo


def _dot_exact01(x, m01):
    hi, mid, lo = _split3(x)
    d = lambda a: jnp.dot(a, m01, preferred_element_type=F32)
    return d(hi) + d(mid) + d(lo)


def _ffn_body(h_ref, g_ref, w1_ref, w3_ref, w2_ref, fg_ref, o_ref, xn_ref, acc_ref, *, n_f, final):
    j = pl.program_id(1)

    @pl.when(j == 0)
    def _():
        xn_ref[...] = _rms(h_ref[...], g_ref[...]).astype(BF16)

    xn = xn_ref[...]
    a = jnp.dot(xn, w1_ref[0], preferred_element_type=F32)
    b = jnp.dot(xn, w3_ref[0], preferred_element_type=F32)
    gated = (a * jax.nn.sigmoid(a) * b).astype(BF16)
    part = jnp.dot(gated, w2_ref[0], preferred_element_type=F32)

    @pl.when(j == 0)
    def _():
        acc_ref[...] = part

    @pl.when(j > 0)
    def _():
        acc_ref[...] += part

    @pl.when(j == n_f - 1)
    def _():
        y = h_ref[...] + 0.5 * acc_ref[...]
        if final:
            y = _rms(y, fg_ref[...])
        o_ref[...] = y


def _ffn(h, gain, w1, w3, w2, layer, final_gain=None, *, tm=512, tf=D_FF):
    m, d = h.shape
    n_f = D_FF // tf
    final = final_gain is not None
    fg = final_gain if final else gain
    body = functools.partial(_ffn_body, n_f=n_f, final=final)
    return pl.pallas_call(
        body,
        grid=(m // tm, n_f),
        in_specs=[
            pl.BlockSpec((tm, d), lambda i, j: (i, 0)),
            pl.BlockSpec((1, d), lambda i, j: (0, 0)),
            pl.BlockSpec((1, d, tf), lambda i, j: (layer, 0, j), pipeline_mode=pl.Buffered(1)),
            pl.BlockSpec((1, d, tf), lambda i, j: (layer, 0, j), pipeline_mode=pl.Buffered(1)),
            pl.BlockSpec((1, tf, d), lambda i, j: (layer, j, 0), pipeline_mode=pl.Buffered(1)),
            pl.BlockSpec((1, d), lambda i, j: (0, 0)),
        ],
        out_specs=pl.BlockSpec((tm, d), lambda i, j: (i, 0)),
        out_shape=jax.ShapeDtypeStruct((m, d), F32),
        scratch_shapes=[pltpu.VMEM((tm, d), BF16), pltpu.VMEM((tm, d), F32)],
        compiler_params=_cparams(("parallel", "arbitrary")),
        name="ffn",
    )(h, gain, w1, w3, w2, fg)


def _inproj_body(h_ref, g_ref, w_ref, wt_ref, heads_ref, xr_ref, gr_ref, gates_ref, kt_ref, gt_ref, gn_ref, vt_ref,
                 *, tm):
    xn = _rms(h_ref[0], g_ref[...]).astype(BF16)
    per = 2 * LANES // HEAD_DIM
    for c in range(N_SLOTS // per):
        u = jnp.dot(xn, w_ref[0, :, c * 2 * LANES:(c + 1) * 2 * LANES], preferred_element_type=F32)
        for i in range(per):
            heads_ref[0, c * per + i] = u[:, i * HEAD_DIM:(i + 1) * HEAD_DIM]
    base = N_SLOTS * HEAD_DIM
    xr_ref[0] = jnp.dot(xn, w_ref[0, :, base:base + LRU_WIDTH], preferred_element_type=F32)
    gr_ref[0] = jnp.dot(xn, w_ref[0, :, base + LRU_WIDTH:base + 2 * LRU_WIDTH], preferred_element_type=F32)
    gates_ref[0] = jnp.dot(xn, w_ref[0, :, base + 2 * LRU_WIDTH:], preferred_element_type=F32)
    ut = lax.dot_general(wt_ref[0], xn, (((1,), (1,)), ((), ())), preferred_element_type=F32)
    for cc in range(tm // ML_CHUNK):
        cols = slice(cc * ML_CHUNK, (cc + 1) * ML_CHUNK)
        for hh in range(ML_HEADS):
            kt_ref[0, hh, cc] = ut[hh * HEAD_DIM:(hh + 1) * HEAD_DIM, cols]
        gt_ref[0, cc] = ut[T_GATE:T_NSAG, cols]
    gn_ref[0] = ut[T_NSAG:T_VS, :]
    for j in range(2 * NSA_GROUPS):
        vt_ref[0, j] = ut[T_VS + j * HEAD_DIM:T_VS + (j + 1) * HEAD_DIM, :]


def _in_proj(h3, gain, w_main, w_t, layer, *, tm=512):
    b, s, d = h3.shape
    n_main = w_main.shape[-1]
    nc = s // ML_CHUNK
    body = functools.partial(_inproj_body, tm=tm)
    return pl.pallas_call(
        body,
        grid=(b, s // tm),
        in_specs=[
            pl.BlockSpec((1, tm, d), lambda bi, ti: (bi, ti, 0)),
            pl.BlockSpec((1, d), lambda bi, ti: (0, 0)),
            pl.BlockSpec((1, d, n_main), lambda bi, ti: (layer, 0, 0)),
            pl.BlockSpec((1, N_T_ROWS, d), lambda bi, ti: (layer, 0, 0)),
        ],
        out_specs=[
            pl.BlockSpec((1, N_SLOTS, tm, HEAD_DIM), lambda bi, ti: (bi, 0, ti, 0)),
            pl.BlockSpec((1, tm, LRU_WIDTH), lambda bi, ti: (bi, ti, 0)),
            pl.BlockSpec((1, tm, LRU_WIDTH), lambda bi, ti: (bi, ti, 0)),
            pl.BlockSpec((1, tm, LANES), lambda bi, ti: (bi, ti, 0)),
            pl.BlockSpec((1, ML_HEADS, tm // ML_CHUNK, HEAD_DIM, ML_CHUNK), lambda bi, ti: (bi, 0, ti, 0, 0)),
            pl.BlockSpec((1, tm // ML_CHUNK, 2 * ML_HEADS, ML_CHUNK), lambda bi, ti: (bi, ti, 0, 0)),
            pl.BlockSpec((1, NSA_HEADS * 3, tm), lambda bi, ti: (bi, 0, ti)),
            pl.BlockSpec((1, 2 * NSA_GROUPS, HEAD_DIM, tm), lambda bi, ti: (bi, 0, 0, ti)),
        ],
        out_shape=[
            jax.ShapeDtypeStruct((b, N_SLOTS, s, HEAD_DIM), F32),
            jax.ShapeDtypeStruct((b, s, LRU_WIDTH), F32),
            jax.ShapeDtypeStruct((b, s, LRU_WIDTH), F32),
            jax.ShapeDtypeStruct((b, s, LANES), F32),
            jax.ShapeDtypeStruct((b, ML_HEADS, nc, HEAD_DIM, ML_CHUNK), F32),
            jax.ShapeDtypeStruct((b, nc, 2 * ML_HEADS, ML_CHUNK), F32),
            jax.ShapeDtypeStruct((b, NSA_HEADS * 3, s), F32),
            jax.ShapeDtypeStruct((b, 2 * NSA_GROUPS, HEAD_DIM, s), F32),
        ],
        compiler_params=_cparams(("parallel", "parallel")),
        name="in_proj",
    )(h3, gain, w_main, w_t)


def _compress_hidden(z_ref, pos_ref, w1_ref):
    n = z_ref.shape[2] // CMP_STRIDE
    top = bot = None
    for c in range(CMP_STRIDE):
        zc = z_ref[0, 0, pl.ds(c, n, stride=CMP_STRIDE), :]
        for half in range(2):
            off = half * CMP_STRIDE + c
            x = (zc + pos_ref[0, off:off + 1, :]).astype(BF16)
            part = jnp.dot(x, w1_ref[0, off * HEAD_DIM:(off + 1) * HEAD_DIM, :], preferred_element_type=F32)
            if half == 0:
                top = part if top is None else top + part
            else:
                bot = part if bot is None else bot + part
    hid = top + pltpu.roll(bot, n - 1, axis=0)
    return (hid * jax.nn.sigmoid(hid)).astype(BF16)


def _compress_body(zk_ref, zv_ref, pos_ref, w1_ref, w2k_ref, w2vt_ref, kc_ref, vct_ref):
    hk = _compress_hidden(zk_ref, pos_ref.at[0:1], w1_ref.at[0:1])
    hv = _compress_hidden(zv_ref, pos_ref.at[1:2], w1_ref.at[1:2])
    kc = jnp.dot(hk, w2k_ref[...], preferred_element_type=F32)
    n = kc.shape[0]
    blk_start = lax.broadcasted_iota(jnp.int32, (n, HEAD_DIM), 0) * CMP_STRIDE
    kc_ref[0, 0] = jnp.concatenate([kc, _position_columns(blk_start, HEAD_DIM)], axis=1).astype(BF16)
    vct = lax.dot_general(w2vt_ref[...], hv, (((1,), (1,)), ((), ())), preferred_element_type=F32)
    vct_ref[0, 0] = vct.astype(BF16)


def _compress(heads, pos, w1, w2k, w2vt):
    b, _, s, _ = heads.shape
    n = s // CMP_STRIDE
    slot = lambda first: pl.BlockSpec((1, 1, s, HEAD_DIM), lambda bi, g: (bi, first + g, 0, 0))
    whole = lambda a: pl.BlockSpec(a.shape, lambda bi, g: (0,) * a.ndim)
    return pl.pallas_call(
        _compress_body,
        grid=(b, NSA_GROUPS),
        in_specs=[slot(SLOT_KC), slot(SLOT_VC), whole(pos), whole(w1), whole(w2k), whole(w2vt)],
        out_specs=[
            pl.BlockSpec((1, 1, n, 2 * HEAD_DIM), lambda bi, g: (bi, g, 0, 0)),
            pl.BlockSpec((1, 1, HEAD_DIM, n), lambda bi, g: (bi, g, 0, 0)),
        ],
        out_shape=[
            jax.ShapeDtypeStruct((b, NSA_GROUPS, n, 2 * HEAD_DIM), BF16),
            jax.ShapeDtypeStruct((b, NSA_GROUPS, HEAD_DIM, n), BF16),
        ],
        compiler_params=_cparams(("parallel", "parallel")),
        name="compress",
    )(heads, heads, pos, w1, w2k, w2vt)


TQ = 256
N_SEL = 32


N_IDX = 3
SEL_KEYS = 512
WIN_KEYS = WINDOW + TQ
V_ROWS = HEAD_DIM + 16


def _position_columns(pos, width):
    lane = lax.broadcasted_iota(jnp.int32, (pos.shape[0], width), 1)
    lo = jnp.bitwise_and(pos, LANES - 1).astype(F32)
    hi = jnp.right_shift(pos, int(np.log2(LANES))).astype(F32)
    return jnp.where(lane < N_IDX, lo, jnp.where(lane < 2 * N_IDX, hi, 0.0))


def _nsa_body(q_ref, sl_ref, kc_ref, vct_ref, ks_ref, kw_ref, vst_ref, vwt_ref, gn_ref, ovt_ref, ext_ref,
              o_ref, ksa_sc, kwa_sc, vs_sc, vw_sc):
    g = pl.program_id(1)
    qi = pl.program_id(2)
    s_len = ks_ref.shape[2]
    r_cols = NSA_HPG * TQ
    nt = (((1,), (1,)), ((), ()))

    @pl.when(qi == 0)
    def _():
        for c in range(s_len // TQ):
            rows = slice(c * TQ, (c + 1) * TQ)
            pos = c * TQ + lax.broadcasted_iota(jnp.int32, (TQ, HEAD_DIM), 0)
            idx = _position_columns(pos, HEAD_DIM)
            ksa_sc[rows, :] = jnp.concatenate([ks_ref[0, 0, rows, :], idx], axis=1).astype(BF16)
            kwa_sc[rows, :] = jnp.concatenate([kw_ref[0, 0, rows, :], idx], axis=1).astype(BF16)
        ones_rows = jnp.where(lax.broadcasted_iota(jnp.int32, (V_ROWS - HEAD_DIM, s_len), 0) == 0, 1.0, 0.0)
        vs_sc[...] = jnp.concatenate([vst_ref[0, 0], ones_rows], axis=0).astype(BF16)
        vw_sc[...] = jnp.concatenate([vwt_ref[0, 0], ones_rows], axis=0).astype(BF16)

    q = q_ref[0].reshape(r_cols, HEAD_DIM) * (HEAD_DIM ** -0.5 * LOG2E)
    q_aug = jnp.concatenate([q, sl_ref[0]], axis=1).astype(BF16)

    def tile4(x):
        return jnp.concatenate([x] * NSA_HPG, axis=1)

    def rel_pos(n_keys, start):
        k_loc = lax.broadcasted_iota(jnp.int32, (n_keys, TQ), 0)
        q_loc = lax.broadcasted_iota(jnp.int32, (n_keys, TQ), 1)
        return (q_loc - k_loc) + (qi * TQ - start)

    n_cmp = kc_ref.shape[2]
    k_loc = lax.broadcasted_iota(jnp.int32, (n_cmp, TQ), 0)
    q_loc = lax.broadcasted_iota(jnp.int32, (n_cmp, TQ), 1)
    s_c = lax.dot_general(kc_ref[0, 0], q_aug, nt, preferred_element_type=F32)
    ok_c = tile4(k_loc * CMP_STRIDE + (CMP_BLOCK - 1) <= qi * TQ + q_loc)
    s_c = jnp.where(ok_c, s_c, NEG)
    m_c = jnp.max(s_c, axis=0, keepdims=True)
    e_c = jnp.where(ok_c, jnp.exp2(s_c - m_c), 0.0)
    l_c = jnp.sum(e_c, axis=0, keepdims=True)
    p_c = e_c / jnp.where(l_c > 0.0, l_c, 1.0)
    o_c = jnp.dot(vct_ref[0, 0], p_c.astype(BF16), preferred_element_type=F32)

    p_sum = p_c[:, 0:TQ]
    for r in range(1, NSA_HPG):
        p_sum = p_sum + p_c[:, r * TQ:(r + 1) * TQ]
    ovt = ovt_ref[...]
    imp = sum(jnp.dot(ovt, part, preferred_element_type=F32) for part in _split3(p_sum))[:N_SEL]
    blk = lax.broadcasted_iota(jnp.int32, (N_SEL, TQ), 0)
    cur = jnp.right_shift(qi * TQ + lax.broadcasted_iota(jnp.int32, (N_SEL, TQ), 1), int(np.log2(SEL_BLOCK)))
    valid = blk <= cur
    forced = valid & ((blk == 0) | (blk > cur - SEL_LOCAL))
    score = jnp.where(valid, imp + jnp.where(forced, FORCE_BONUS, 0.0), NEG)
    rank = jnp.zeros((N_SEL, TQ), F32)
    for jp in range(N_SEL):
        row = score[jp:jp + 1, :]
        beats = (row > score) | ((row == score) & (blk > jp))
        rank = rank + jnp.where(beats, 1.0, 0.0)
    unsel = jnp.where(rank < float(SEL_TOPK), 0.0, 1.0)
    unsel = jnp.concatenate([unsel, jnp.zeros((LANES - N_SEL, TQ), F32)], axis=0).astype(BF16)

    def attend(ka_sc, vt_sc, start, n_keys, carry, extra=None, keep=None):
        m, acc = carry
        s = lax.dot_general(ka_sc[pl.ds(start, n_keys), :], q_aug, nt, preferred_element_type=F32)
        if extra is not None:
            s = s + tile4(extra)
        if keep is not None:
            s = jnp.where(tile4(keep), s, NEG)
        m_new = jnp.maximum(m, jnp.max(s, axis=0, keepdims=True))
        p = jnp.exp2((s - m_new).astype(BF16))
        pv = jnp.dot(vt_sc[:, pl.ds(start, n_keys)], p, preferred_element_type=F32)
        return m_new, jnp.exp2(m - m_new) * acc + pv

    def normalise(acc):
        return acc[:HEAD_DIM] / acc[HEAD_DIM:HEAD_DIM + 1]

    init = (jnp.full((1, r_cols), NEG, F32), jnp.zeros((V_ROWS, r_cols), F32))

    def sel_step(j, carry, last, n_keys=SEL_KEYS):
        start = pl.multiple_of(j * SEL_KEYS, SEL_KEYS)
        bias = jnp.dot(ext_ref[pl.ds(start, n_keys), :], unsel, preferred_element_type=F32)
        keep = rel_pos(n_keys, start) >= 0 if last else None
        return attend(ksa_sc, vs_sc, start, n_keys, carry, extra=bias, keep=keep)

    n_full = (qi * TQ) // SEL_KEYS
    carry = lax.fori_loop(0, n_full, lambda j, c: sel_step(j, c, False), init)
    o_s = normalise(sel_step(n_full, carry, True)[1])

    w_start = pl.multiple_of(jnp.maximum(qi * TQ - WINDOW, 0), TQ)
    rel = rel_pos(WIN_KEYS, w_start)
    o_w = normalise(attend(kwa_sc, vw_sc, w_start, WIN_KEYS, init, keep=(rel >= 0) & (rel < WINDOW))[1])

    gates = jax.nn.sigmoid(gn_ref[0])
    outs = []
    for r in range(NSA_HPG):
        cols = slice(r * TQ, (r + 1) * TQ)
        acc = None
        for br, o_b in enumerate((o_c, o_s, o_w)):
            row_lo = r * 3 + br
            g0 = gates[row_lo:row_lo + 1, :]
            g1 = gates[NSA_HPG * 3 + row_lo:NSA_HPG * 3 + row_lo + 1, :]
            term = jnp.where(g == 0, g0, g1) * o_b[:, cols]
            acc = term if acc is None else acc + term
        outs.append(acc)
    o_ref[0] = jnp.concatenate(outs, axis=0).T


def _nsa(heads, kc_aug, vct, vt, gn, sl, ovt, ext):
    b, _, s, _ = heads.shape
    n_cmp = kc_aug.shape[2]
    assert n_cmp == LANES and s // SEL_BLOCK == N_SEL and s % SEL_KEYS == 0
    r_cols = NSA_HPG * TQ
    k_spec = lambda slot: pl.BlockSpec((1, 1, s, HEAD_DIM), lambda bi, g, qi: (bi, slot + g, 0, 0))
    vt_spec = lambda first: pl.BlockSpec((1, 1, HEAD_DIM, s), lambda bi, g, qi: (bi, first + g, 0, 0))
    return pl.pallas_call(
        _nsa_body,
        grid=(b, NSA_GROUPS, s // TQ),
        in_specs=[
            pl.BlockSpec((1, NSA_HPG, TQ, HEAD_DIM), lambda bi, g, qi: (bi, g, qi, 0)),
            pl.BlockSpec((1, r_cols, HEAD_DIM), lambda bi, g, qi: (g, 0, 0)),
            pl.BlockSpec((1, 1, n_cmp, 2 * HEAD_DIM), lambda bi, g, qi: (bi, g, 0, 0)),
            pl.BlockSpec((1, 1, HEAD_DIM, n_cmp), lambda bi, g, qi: (bi, g, 0, 0)),
            k_spec(SLOT_KS), k_spec(SLOT_KW), vt_spec(0), vt_spec(NSA_GROUPS),
            pl.BlockSpec((1, NSA_HEADS * 3, TQ), lambda bi, g, qi: (bi, 0, qi)),
            pl.BlockSpec((LANES, LANES), lambda bi, g, qi: (0, 0)),
            pl.BlockSpec((s, LANES), lambda bi, g, qi: (0, 0)),
        ],
        out_specs=pl.BlockSpec((1, TQ, NSA_HPG * HEAD_DIM), lambda bi, g, qi: (bi, qi, g)),
        out_shape=jax.ShapeDtypeStruct((b, s, NSA_WIDTH), F32),
        scratch_shapes=[
            pltpu.VMEM((s, 2 * HEAD_DIM), BF16),
            pltpu.VMEM((s, 2 * HEAD_DIM), BF16),
            pltpu.VMEM((V_ROWS, s), BF16),
            pltpu.VMEM((V_ROWS, s), BF16),
        ],
        compiler_params=_cparams(("arbitrary", "arbitrary", "arbitrary")),
        name="nsa",
    )(heads, sl, kc_aug, vct, heads, heads, vt, vt, gn, ovt, ext)


SCAN_PAD = 1024
SCAN_ROWS = 256


def _lru_body(x_ref, g_ref, cw_ref, cb_ref, wa_ref, ba_ref, wx_ref, bx_ref, lam_ref, o_ref,
              a0, b0, a1, b1):
    s = x_ref.shape[1]
    x = x_ref[0]
    row = lax.broadcasted_iota(jnp.int32, x.shape, 0)
    xc = cb_ref[...]
    for tap in range(CONV_WIDTH):
        back = CONV_WIDTH - 1 - tap
        xs = x if back == 0 else jnp.where(row >= back, pltpu.roll(x, back, axis=0), 0.0)
        xc = xc + xs * cw_ref[tap:tap + 1, :]
    xb = xc.astype(BF16)
    r = jax.nn.sigmoid(jnp.dot(xb, wa_ref[...], preferred_element_type=F32) + ba_ref[...])
    i = jax.nn.sigmoid(jnp.dot(xb, wx_ref[...], preferred_element_type=F32) + bx_ref[...])
    log_a = -LRU_C * r * jax.nn.softplus(-lam_ref[...])
    a = jnp.exp(log_a)
    u = jnp.sqrt(-jnp.tanh(log_a) * (a * a + 1.0)) * (i * xc)

    bufs = ((a0, b0), (a1, b1))
    for ab, bb in bufs:
        ab[0:SCAN_PAD, :] = jnp.ones((SCAN_PAD, x.shape[1]), F32)
        bb[0:SCAN_PAD, :] = jnp.zeros((SCAN_PAD, x.shape[1]), F32)
    a0[SCAN_PAD:, :] = a
    b0[SCAN_PAD:, :] = u
    n_steps = int(np.log2(s))
    for k in range(n_steps):
        d = 1 << k
        (sa, sb), (da, db) = bufs[k % 2], bufs[(k + 1) % 2]
        for c in range(s // SCAN_ROWS):
            lo = SCAN_PAD + c * SCAN_ROWS
            a_cur = sa[lo:lo + SCAN_ROWS, :]
            a_prev = sa[lo - d:lo - d + SCAN_ROWS, :]
            b_cur = sb[lo:lo + SCAN_ROWS, :]
            b_prev = sb[lo - d:lo - d + SCAN_ROWS, :]
            da[lo:lo + SCAN_ROWS, :] = a_cur * a_prev
            db[lo:lo + SCAN_ROWS, :] = a_cur * b_prev + b_cur
    h = bufs[n_steps % 2][1][SCAN_PAD:, :]
    o_ref[0] = h * jax.nn.gelu(g_ref[0])


def _rglru(xr, gr, conv_w, conv_b, wa, ba, wx, bx, lam):
    b, s, c = xr.shape
    assert s <= 2 * SCAN_PAD and s % SCAN_ROWS == 0
    tok = pl.BlockSpec((1, s, c), lambda bi: (bi, 0, 0))
    vec = lambda n: pl.BlockSpec((n, c), lambda bi: (0, 0))
    return pl.pallas_call(
        _lru_body,
        grid=(b,),
        in_specs=[tok, tok, vec(CONV_WIDTH), vec(1), vec(c), vec(1), vec(c), vec(1), vec(1)],
        out_specs=tok,
        out_shape=jax.ShapeDtypeStruct((b, s, c), F32),
        scratch_shapes=[pltpu.VMEM((SCAN_PAD + s, c), F32)] * 4,
        compiler_params=_cparams(("parallel",)),
        name="rglru",
    )(xr, gr, conv_w, conv_b, wa, ba, wx, bx, lam)


def _mlstm_body(q_ref, v_ref, o_ref_in, kt_ref, gcol_ref, grow_ref, bcol_ref, brow_ref, tri_ref, out_ref):
    nc = q_ref.shape[2]
    L = ML_CHUNK
    ones_col = jnp.where(lax.broadcasted_iota(jnp.int32, (L, HEAD_DIM), 1) == 0, 1.0, 0.0)
    rr = lax.broadcasted_iota(jnp.int32, (L, L), 0)
    cc = lax.broadcasted_iota(jnp.int32, (L, L), 1)
    causal = cc <= rr
    srow = lax.broadcasted_iota(jnp.int32, (L, LANES), 0)

    def chunk(c, carry):
        gcol = gcol_ref[0, c] + bcol_ref[...]
        lf_col = jax.nn.log_sigmoid(gcol)
        a_col = lf_col
        for k in range(int(np.log2(L))):
            d = 1 << k
            a_col = a_col + jnp.where(srow >= d, pltpu.roll(a_col, d, axis=0), 0.0)
        grow = grow_ref[0, c] + brow_ref[...]
        lf_row = jax.nn.log_sigmoid(grow)
        a_rows = _dot_exact01(lf_row, tri_ref[...])
        new_carry = []
        outs = []
        for h in range(ML_HEADS):
            c_aug, m_prev = carry[h]
            q = q_ref[0, h, c].astype(BF16)
            kt = kt_ref[0, h, c]
            v = v_ref[0, h, c]
            v_aug = jnp.concatenate([v, ones_col], axis=1).astype(BF16)
            a_c = a_col[:, GATE_F + h:GATE_F + h + 1]
            a_r = jnp.broadcast_to(a_rows[ML_HEADS + h:ML_HEADS + h + 1, :], (SUBLANES, L))
            ig_r = jnp.broadcast_to(grow[h:h + 1, :], (SUBLANES, L))
            a_end = a_r[:, L - 1:L]
            tall = lambda s8: jnp.concatenate([s8] * (L // SUBLANES), axis=0)

            rhs = jnp.concatenate([c_aug.astype(BF16), kt.astype(BF16)], axis=1)
            q_both = jnp.dot(q, rhs, preferred_element_type=F32)
            q_state = q_both[:, :2 * HEAD_DIM]
            qk = q_both[:, 2 * HEAD_DIM:] * HEAD_DIM ** -0.5
            dmat = jnp.where(causal, a_c + (ig_r - a_r)[0:1, :], NEG)
            m_inter = a_c + tall(m_prev)
            m = jnp.maximum(m_inter, jnp.max(dmat, axis=1, keepdims=True))
            inter = jnp.exp(m_inter - m)
            w = jnp.exp(dmat - m) * qk
            num = inter * q_state + jnp.dot(w.astype(BF16), v_aug, preferred_element_type=F32)
            den = num[:, HEAD_DIM:HEAD_DIM + 1]
            hval = num[:, :HEAD_DIM] / jnp.maximum(jnp.abs(den), jnp.exp(-m))
            outs.append(jax.nn.sigmoid(o_ref_in[0, h, c]) * hval)

            w_end = a_end - a_r + ig_r
            m_new = jnp.maximum(a_end + m_prev, jnp.max(w_end, axis=1, keepdims=True))
            decay = jnp.exp(a_end + m_prev - m_new)
            wk = jnp.exp(w_end - m_new) * HEAD_DIM ** -0.5
            kw = (kt * wk[0:1, :]).astype(BF16)
            c_new = tall(decay) * c_aug + jnp.dot(kw, v_aug, preferred_element_type=F32)
            new_carry.append((c_new, m_new))
        out_ref[0, c] = jnp.concatenate(outs, axis=1)
        return tuple(new_carry)

    init = tuple((jnp.zeros((HEAD_DIM, 2 * HEAD_DIM), F32), jnp.zeros((SUBLANES, 1), F32)) for _ in range(ML_HEADS))
    lax.fori_loop(0, nc, chunk, init, unroll=2)


def _mlstm(heads_c, kt, gates_c, gt, bcol, brow, tri):
    b, _, nc, L, _ = heads_c.shape
    slot_spec = lambda slot: pl.BlockSpec((1, ML_HEADS, nc, L, HEAD_DIM), lambda bi: (bi, slot // ML_HEADS, 0, 0, 0))
    return pl.pallas_call(
        _mlstm_body,
        grid=(b,),
        in_specs=[
            slot_spec(SLOT_QM), slot_spec(SLOT_VM), slot_spec(SLOT_OM),
            pl.BlockSpec((1, ML_HEADS, nc, HEAD_DIM, L), lambda bi: (bi, 0, 0, 0, 0)),
            pl.BlockSpec((1, nc, L, LANES), lambda bi: (bi, 0, 0, 0)),
            pl.BlockSpec((1, nc, 2 * ML_HEADS, L), lambda bi: (bi, 0, 0, 0)),
            pl.BlockSpec((1, LANES), lambda bi: (0, 0)),
            pl.BlockSpec((2 * ML_HEADS, 1), lambda bi: (0, 0)),
            pl.BlockSpec((L, L), lambda bi: (0, 0)),
        ],
        out_specs=pl.BlockSpec((1, nc, L, ML_WIDTH), lambda bi: (bi, 0, 0, 0)),
        out_shape=jax.ShapeDtypeStruct((b, nc, L, ML_WIDTH), F32),
        compiler_params=_cparams(("parallel",)),
        name="mlstm",
    )(heads_c, heads_c, heads_c, kt, gates_c, gt, bcol, brow, tri)


def _outproj_body(h_ref, on_ref, ol_ref, om_ref, gn_ref, ind_ref, indt_ref, w_ref, o_ref):
    acc = h_ref[...]
    col = 0
    for x_ref in (on_ref, ol_ref, om_ref):
        x = x_ref[...]
        width = x.shape[1]
        ind = ind_ref[col:col + width, :]
        ms = jnp.dot((x * x).astype(BF16), ind, preferred_element_type=F32) * (1.0 / HEAD_DIM)
        r_hi, r_mid, _ = _split3(lax.rsqrt(ms + EPS))
        indt = indt_ref[:, col:col + width]
        scale = (jnp.dot(r_hi, indt, preferred_element_type=F32)
                 + jnp.dot(r_mid, indt, preferred_element_type=F32))
        xn = (x * scale * gn_ref[:, col:col + width]).astype(BF16)
        acc = acc + jnp.dot(xn, w_ref[0, col:col + width, :], preferred_element_type=F32)
        col += width
    o_ref[...] = acc


def _out_proj(h, o_nsa, o_lru, o_ml, gain, ind, indt, w_out, layer, *, tm=512):
    m, d = h.shape
    tok = lambda w: pl.BlockSpec((tm, w), lambda i: (i, 0))
    return pl.pallas_call(
        _outproj_body,
        grid=(m // tm,),
        in_specs=[
            tok(d), tok(o_nsa.shape[1]), tok(o_lru.shape[1]), tok(o_ml.shape[1]),
            pl.BlockSpec((1, d), lambda i: (0, 0)),
            pl.BlockSpec((d, LANES), lambda i: (0, 0)),
            pl.BlockSpec((LANES, d), lambda i: (0, 0)),
            pl.BlockSpec((1, d, d), lambda i: (layer, 0, 0)),
        ],
        out_specs=tok(d),
        out_shape=jax.ShapeDtypeStruct((m, d), F32),
        compiler_params=_cparams(("parallel",)),
        name="out_proj",
    )(h, o_nsa, o_lru, o_ml, gain, ind, indt, w_out)


def _pack_w_in(w_in):
    wb = w_in.astype(BF16)
    cols = lambda lo, hi: wb[:, :, lo:hi]
    pad = jnp.zeros(wb.shape[:2] + (LANES - (NSA_HEADS * 3 + 2 * ML_HEADS),), BF16)
    main = jnp.concatenate([
        cols(_OFF_Q, _OFF_G),
        cols(_OFF_QM, _OFF_IM),
        cols(_OFF_OM, D_IN),
        cols(_OFF_XR, _OFF_QM),
        cols(_OFF_G, _OFF_XR),
        cols(_OFF_IM, _OFF_OM),
        pad,
    ], axis=2)
    off_vs = _OFF_KV + 3 * NSA_KV_WIDTH
    off_vw = _OFF_KV + 5 * NSA_KV_WIDTH
    w_t = jnp.concatenate([cols(_OFF_KM, _OFF_VM), cols(_OFF_IM, _OFF_OM), cols(_OFF_G, _OFF_XR),
                           cols(off_vs, off_vs + NSA_KV_WIDTH), cols(off_vw, off_vw + NSA_KV_WIDTH)], axis=2)
    return main, jnp.swapaxes(w_t, 1, 2)


def _block_diag(w):
    depth, n, c, _ = w.shape
    eye = jnp.eye(n, dtype=w.dtype)
    return jnp.einsum("lncd,nm->lncmd", w, eye).reshape(depth, n * c, n * c)


def _constants(s):
    n_cmp_pad = s // CMP_STRIDE
    n = np.arange(n_cmp_pad)
    j = np.arange(LANES)
    cmp_start = n * CMP_STRIDE
    cmp_end = cmp_start + CMP_BLOCK - 1
    slc_start = j * SEL_BLOCK
    n_cmp = (s - CMP_BLOCK) // CMP_STRIDE + 1
    ov = ((cmp_start[:, None] <= slc_start[None, :] + SEL_BLOCK - 1) & (cmp_end[:, None] >= slc_start[None, :])
          & (n[:, None] < n_cmp) & (j[None, :] < s // SEL_BLOCK))
    key_block = np.arange(s) // SEL_BLOCK
    ext = np.where(key_block[:, None] == j[None, :], NEG, 0.0)
    tri = np.triu(np.ones((ML_CHUNK, ML_CHUNK)))
    heads = np.arange(D_MODEL) // HEAD_DIM
    ind = (heads[:, None] == j[None, :])
    slope = 2.0 ** (-8.0 * np.arange(1, NSA_HEADS + 1) / NSA_HEADS) * LOG2E
    parts = jnp.stack(_split3(jnp.asarray(slope, F32)), axis=1).astype(F32)
    cols = jnp.concatenate([parts, parts * float(LANES)], axis=1)
    cols = jnp.pad(cols, ((0, 0), (0, HEAD_DIM - 2 * N_IDX)))
    sl = jnp.repeat(cols, TQ, axis=0).reshape(NSA_GROUPS, NSA_HPG * TQ, HEAD_DIM)
    return dict(ovt=jnp.asarray(ov.T, BF16), ext=jnp.asarray(ext, BF16), tri=jnp.asarray(tri, BF16),
                ind=jnp.asarray(ind, BF16), indt=jnp.asarray(ind.T, BF16), sl=sl)


def _mixers(h3, layer, p):
    b, s, d = h3.shape
    heads, xr, gr, gates, kt, gt, gn, vt = _in_proj(h3, p["mix_norm"][layer][None], p["w_main"], p["w_t"], layer)

    kc_aug, vct = _compress(heads, p["cmp_pos"][layer], p["cmp_w1"][layer], p["cmp_w2k"][layer], p["cmp_w2vt"][layer])
    o_nsa = _nsa(heads, kc_aug, vct, vt, gn, p["sl"], p["ovt"], p["ext"])

    o_lru = _rglru(xr, gr, p["conv_w"][layer], p["conv_b"][layer][None], p["wa"][layer], p["ba"][layer][None],
                   p["wx"][layer], p["bx"][layer][None], p["lam"][layer][None])

    nc = s // ML_CHUNK
    heads_c = heads.reshape(b, N_SLOTS, nc, ML_CHUNK, HEAD_DIM)
    gates_c = gates.reshape(b, nc, ML_CHUNK, LANES)
    o_ml = _mlstm(heads_c, kt, gates_c, gt, p["ml_bcol"][layer][None], p["ml_brow"][layer][:, None], p["tri"])
    m = b * s
    return o_nsa.reshape(m, NSA_WIDTH), o_lru.reshape(m, LRU_WIDTH), o_ml.reshape(m, ML_WIDTH)


def kernel(x, ffn1_norm, ffn1_w1, ffn1_w3, ffn1_w2, mix_norm, w_in, nsa_cmp_pos_k, nsa_cmp_w1_k, nsa_cmp_w2_k, nsa_cmp_pos_v, nsa_cmp_w1_v, nsa_cmp_w2_v, lru_conv_w, lru_conv_b, lru_w_a, lru_b_a, lru_w_x, lru_b_x, lru_lambda, ml_b_i, ml_b_f, head_norm, w_out, ffn2_norm, ffn2_w1, ffn2_w3, ffn2_w2, final_norm):
    b, s, d = x.shape
    m = b * s
    depth = w_in.shape[0]
    consts = _constants(s)
    ind, indt = consts["ind"], consts["indt"]
    w_main, w_t = _pack_w_in(w_in)
    zeros_gate = jnp.zeros((depth, GATE_I), F32)
    bias_lanes = jnp.concatenate([zeros_gate, ml_b_i, ml_b_f], axis=1)
    p = {
        **consts,
        "mix_norm": mix_norm, "w_main": w_main, "w_t": w_t,
        "cmp_pos": jnp.stack([nsa_cmp_pos_k, nsa_cmp_pos_v], axis=1),
        "cmp_w1": jnp.stack([nsa_cmp_w1_k, nsa_cmp_w1_v], axis=1).astype(BF16),
        "cmp_w2k": nsa_cmp_w2_k.astype(BF16),
        "cmp_w2vt": jnp.swapaxes(nsa_cmp_w2_v, 1, 2).astype(BF16),
        "conv_w": lru_conv_w, "conv_b": lru_conv_b,
        "wa": _block_diag(lru_w_a).astype(BF16), "ba": lru_b_a.reshape(depth, LRU_WIDTH),
        "wx": _block_diag(lru_w_x).astype(BF16), "bx": lru_b_x.reshape(depth, LRU_WIDTH),
        "lam": lru_lambda,
        "ml_bcol": jnp.pad(bias_lanes, ((0, 0), (0, LANES - bias_lanes.shape[1]))),
        "ml_brow": jnp.concatenate([ml_b_i, ml_b_f], axis=1),
    }
    f1 = [w.astype(BF16) for w in (ffn1_w1, ffn1_w3, ffn1_w2)]
    f2 = [w.astype(BF16) for w in (ffn2_w1, ffn2_w3, ffn2_w2)]
    w_out_b = w_out.astype(BF16)
    gain_heads = head_norm.reshape(depth, 1, d)

    h = x.reshape(m, d)
    for layer in range(depth):
        h = _ffn(h, ffn1_norm[layer][None], *f1, layer)
        o_nsa, o_lru, o_ml = _mixers(h.reshape(b, s, d), layer, p)
        h = _out_proj(h, o_nsa, o_lru, o_ml, gain_heads[layer], ind, indt, w_out_b, layer)
        last = layer == depth - 1
        h = _ffn(h, ffn2_norm[layer][None], *f2, layer, final_norm[None] if last else None)
    return h.reshape(b, s, d)
```
